```python
import math
import jax, jax.numpy as jnp
from jax import lax
import numpy as np

D_MODEL = 1024
BATCH = 16
SEQ = 4096
DEPTH = 4

MEM_LEN = 256
N_EVEN = (DEPTH + 1) // 2
N_ODD = DEPTH // 2
EPS = 1e-6
NEG_BIG = -1e30
TINY = 1e-30

HGRN_HEADS = 4
HGRN_DK = 128
HGRN_DV = 128
HGRN_WIDTH = HGRN_HEADS * HGRN_DK
HGRN_CHUNK = 64
CONV_CH = D_MODEL - HGRN_WIDTH
CONV_WIDTH = 31
EVEN_IN = 4 * HGRN_WIDTH + 2 * CONV_CH
DA_HEADS = 8
DA_HEAD_DIM = 64
DA_VDIM = 2 * DA_HEAD_DIM
ODD_IN = 3 * DA_HEADS * 2 * DA_HEAD_DIM
Q_BLOCK = 128
REL_BUCKETS = 32
REL_MAX_DIST = 128
XA_HEADS = 4
XA_HEAD_DIM = D_MODEL // XA_HEADS
N_GROUPS = 4
EXPERTS_PER_GROUP = 4
N_EXPERTS = N_GROUPS * EXPERTS_PER_GROUP
TOP_K = 2
EXPERT_FF = 512

kernel_name = "hybrid_hgrn2_conv_diffattn_hmoe"


def rms_norm(x, g):
    xf = x.astype(jnp.float32)
    y = xf * lax.rsqrt(jnp.mean(xf * xf, axis=-1, keepdims=True) + EPS)
    return (y * g.astype(jnp.float32)).astype(x.dtype)


def layer_norm(x, g, b):
    xf = x.astype(jnp.float32)
    mu = jnp.mean(xf, axis=-1, keepdims=True)
    var = jnp.mean(jnp.square(xf - mu), axis=-1, keepdims=True)
    y = (xf - mu) * lax.rsqrt(var + EPS)
    return (y * g.astype(jnp.float32) + b.astype(jnp.float32)).astype(x.dtype)


def hgrn2_chunk_scan(q, k, v, log_f):
    B, S, H, DK = q.shape
    DV = v.shape[-1]
    C = HGRN_CHUNK
    n = S // C

    def to_chunks(t):
        return t.astype(jnp.float32).reshape(B, n, C, H, t.shape[-1]).transpose(1, 0, 3, 2, 4)

    qc, kc, vc, gc = to_chunks(q), to_chunks(k), to_chunks(v), to_chunks(log_f)
    causal = jnp.tril(jnp.ones((C, C), dtype=bool))[:, :, None]

    def step(state, inp):
        qi, ki, vi, gi = inp
        b = jnp.cumsum(gi, axis=2)
        o_inter = jnp.einsum('bhtk,bhkv->bhtv', qi * jnp.exp(b), state)
        diff = b[:, :, :, None, :] - b[:, :, None, :, :]
        decay = jnp.where(causal, jnp.exp(jnp.where(causal, diff, 0.0)), 0.0)
        scores = jnp.einsum('bhtk,bhsk,bhtsk->bhts', qi, ki, decay)
        o_intra = jnp.einsum('bhts,bhsv->bhtv', scores, vi)
        b_last = b[:, :, -1:, :]
        new_state = (jnp.exp(b_last[:, :, 0, :])[..., None] * state
                     + jnp.einsum('bhsk,bhsv->bhkv', ki * jnp.exp(b_last - b), vi))
        return new_state, o_inter + o_intra

    state0 = jnp.zeros((B, H, DK, DV), jnp.float32)
    _, o = lax.scan(step, state0, (qc, kc, vc, gc))
    return o.transpose(1, 0, 3, 2, 4).reshape(B, S, H, DV)


def hgrn_conv_mixer(h, w_in, w_out, lb, hg_norm, dw, db, ln_g, ln_b):
    B, S, _ = h.shape
    proj = h @ w_in
    W = HGRN_WIDTH
    zq, zf, zi, zg, cu, cv = jnp.split(
        proj, [W, 2 * W, 3 * W, 4 * W, 4 * W + CONV_CH], axis=-1)
    f = lb + (1.0 - lb) * jax.nn.sigmoid(zf.astype(jnp.float32))
    log_f = jnp.log(jnp.maximum(f, TINY))
    k = 1.0 - f
    q = jax.nn.silu(zq.astype(jnp.float32))
    heads = lambda t: t.reshape(B, S, HGRN_HEADS, -1)
    o = hgrn2_chunk_scan(heads(q), heads(k), heads(zi), heads(log_f))
    o = rms_norm(o, hg_norm).reshape(B, S, HGRN_HEADS * HGRN_DV)
    o_hgrn = (o * jax.nn.sigmoid(zg.astype(jnp.float32))).astype(h.dtype)
    u = cu * jax.nn.sigmoid(cv)
    c = lax.conv_general_dilated(
        u, dw[:, None, :], window_strides=(1,), padding=[(CONV_WIDTH - 1, 0)],
        dimension_numbers=('NWC', 'WIO', 'NWC'), feature_group_count=CONV_CH) + db
    c = jax.nn.silu(layer_norm(c, ln_g, ln_b))
    return jnp.concatenate([o_hgrn, c.astype(h.dtype)], axis=-1) @ w_out


def t5_bucket(rel):
    n = jnp.maximum(rel, 0)
    max_exact = REL_BUCKETS // 2
    nf = jnp.maximum(n, 1).astype(jnp.float32)
    large = max_exact + (jnp.log(nf / max_exact) / math.log(REL_MAX_DIST / max_exact)
                         * (REL_BUCKETS - max_exact)).astype(jnp.int32)
    large = jnp.clip(large, 0, REL_BUCKETS - 1)
    return jnp.where(n < max_exact, n, large)


def diff_attention(h, w_qkv, w_out, lq1, lk1, lq2, lk2, subln_g, rel_bias, lam_init):
    B, S, _ = h.shape
    q, k, v = jnp.split(h @ w_qkv, 3, axis=-1)
    q = q.reshape(B, S, DA_HEADS, 2, DA_HEAD_DIM)
    k = k.reshape(B, S, DA_HEADS, 2, DA_HEAD_DIM)
    v = v.reshape(B, S, DA_HEADS, DA_VDIM)
    f32 = jnp.float32
    lam = (jnp.exp(jnp.sum(lq1.astype(f32) * lk1.astype(f32)))
           - jnp.exp(jnp.sum(lq2.astype(f32) * lk2.astype(f32))) + lam_init)
    nb = S // Q_BLOCK
    qb = q.reshape(B, nb, Q_BLOCK, DA_HEADS, 2, DA_HEAD_DIM).transpose(1, 0, 3, 4, 2, 5)
    kt = k.transpose(0, 2, 3, 1, 4)
    vt = v.transpose(0, 2, 1, 3)
    k_pos = jnp.arange(S)
    table = rel_bias.astype(f32)
    scale = DA_HEAD_DIM ** -0.5

    def block(args):
        qblk, start = args
        q_pos = start + jnp.arange(Q_BLOCK)
        rel = q_pos[:, None] - k_pos[None, :]
        bias = table[t5_bucket(rel)].transpose(2, 0, 1)
        s = jnp.einsum('bhmqd,bhmkd->bhmqk', qblk, kt).astype(f32) * scale + bias[None, :, None]
        s = jnp.where(rel >= 0, s, NEG_BIG)
        p = jax.nn.softmax(s, axis=-1)
        a = p[:, :, 0] - lam * p[:, :, 1]
        return jnp.einsum('bhqk,bhkv->bhqv', a.astype(vt.dtype), vt)

    o = lax.map(block, (qb, jnp.arange(nb) * Q_BLOCK))
    o = o.transpose(1, 0, 3, 2, 4).reshape(B, S, DA_HEADS, DA_VDIM)
    o = rms_norm(o, subln_g) * (1.0 - lam_init)
    return o.reshape(B, S, DA_HEADS * DA_VDIM) @ w_out


def memory_cross_attention(h, mem_n, wq, wkv, wo):
    B, S, _ = h.shape
    M = mem_n.shape[1]
    q = (h @ wq).reshape(B, S, XA_HEADS, XA_HEAD_DIM)
    k, v = jnp.split(mem_n @ wkv, 2, axis=-1)
    k = k.reshape(B, M, XA_HEADS, XA_HEAD_DIM)
    v = v.reshape(B, M, XA_HEADS, XA_HEAD_DIM)
    s = jnp.einsum('bshd,bmhd->bhsm', q, k).astype(jnp.float32) * (XA_HEAD_DIM ** -0.5)
    p = jax.nn.softmax(s, axis=-1)
    o = jnp.einsum('bhsm,bmhd->bshd', p.astype(v.dtype), v).reshape(B, S, D_MODEL)
    return o @ wo


def hierarchical_moe(h, w_grp, b_grp, w_exp, b_exp, w1, w3, w2):
    B, S, D = h.shape
    t = h.reshape(-1, D)
    f32 = jnp.float32
    grp_logits = (t @ w_grp).astype(f32) + b_grp.astype(f32)
    grp_prob = jax.nn.softmax(grp_logits, axis=-1)
    g_sel = jnp.argmax(grp_logits, axis=-1)
    g_w = jnp.take_along_axis(grp_prob, g_sel[:, None], axis=-1)
    exp_logits = ((t @ w_exp).astype(f32) + b_exp.astype(f32)).reshape(-1, N_GROUPS, EXPERTS_PER_GROUP)
    in_grp = jnp.take_along_axis(exp_logits, g_sel[:, None, None], axis=1)[:, 0]
    top_v, top_i = lax.top_k(in_grp, TOP_K)
    top_w = jax.nn.softmax(top_v, axis=-1) * g_w
    expert_id = g_sel[:, None] * EXPERTS_PER_GROUP + top_i
    gate = jnp.sum(jax.nn.one_hot(expert_id, N_EXPERTS, dtype=f32) * top_w[..., None], axis=1)
    gate = gate.astype(t.dtype)
    out = jnp.zeros_like(t)
    for e in range(N_EXPERTS):
        hid = jax.nn.silu(t @ w1[e]) * (t @ w3[e])
        out = out + gate[:, e:e + 1] * (hid @ w2[e])
    return out.reshape(B, S, D)


def setup_inputs(seed: int = 0) -> dict:
    key = jax.random.key(seed)
    ks = iter(jax.random.split(key, 40))
    nrm = lambda shape, scale: jax.random.normal(next(ks), shape, jnp.float32) * scale
    gain = lambda shape: 1.0 + nrm(shape, 0.05)
    D = D_MODEL
    return {
        "x": nrm((BATCH, SEQ, D), 1.0),
        "mem": nrm((BATCH, MEM_LEN, D), 1.0),
        "rel_bias": nrm((REL_BUCKETS, DA_HEADS), 0.5),
        "mem_norm_g": gain((D,)),
        "final_norm_g": gain((D,)),
        "hgrn_lb": nrm((N_EVEN, HGRN_WIDTH), 0.5),
        "norm_mix_g": gain((DEPTH, D)),
        "norm_xattn_g": gain((DEPTH, D)),
        "norm_ffn_g": gain((DEPTH, D)),
        "even_w_in": nrm((N_EVEN, D, EVEN_IN), D ** -0.5),
        "even_w_out": nrm((N_EVEN, HGRN_HEADS * HGRN_DV + CONV_CH, D), (HGRN_HEADS * HGRN_DV + CONV_CH) ** -0.5),
        "hgrn_norm_g": gain((N_EVEN, HGRN_DV)),
        "conv_dw": nrm((N_EVEN, CONV_WIDTH, CONV_CH), CONV_WIDTH ** -0.5),
        "conv_db": nrm((N_EVEN, CONV_CH), 0.02),
        "conv_ln_g": gain((N_EVEN, CONV_CH)),
        "conv_ln_b": nrm((N_EVEN, CONV_CH), 0.02),
        "odd_w_qkv": nrm((N_ODD, D, ODD_IN), D ** -0.5),
        "odd_w_out": nrm((N_ODD, DA_HEADS * DA_VDIM, D), (DA_HEADS * DA_VDIM) ** -0.5),
        "lam_q1": nrm((N_ODD, DA_HEAD_DIM), 0.1),
        "lam_k1": nrm((N_ODD, DA_HEAD_DIM), 0.1),
        "lam_q2": nrm((N_ODD, DA_HEAD_DIM), 0.1),
        "lam_k2": nrm((N_ODD, DA_HEAD_DIM), 0.1),
        "subln_g": gain((N_ODD, DA_VDIM)),
        "xa_wq": nrm((DEPTH, D, D), D ** -0.5),
        "xa_wkv": nrm((DEPTH, D, 2 * D), D ** -0.5),
        "xa_wo": nrm((DEPTH, D, D), D ** -0.5),
        "moe_w_grp": nrm((DEPTH, D, N_GROUPS), D ** -0.5),
        "moe_b_grp": nrm((DEPTH, N_GROUPS), 0.01),
        "moe_w_exp": nrm((DEPTH, D, N_EXPERTS), D ** -0.5),
        "moe_b_exp": nrm((DEPTH, N_EXPERTS), 0.01),
        "moe_w1": nrm((DEPTH, N_EXPERTS, D, EXPERT_FF), D ** -0.5),
        "moe_w3": nrm((DEPTH, N_EXPERTS, D, EXPERT_FF), D ** -0.5),
        "moe_w2": nrm((DEPTH, N_EXPERTS, EXPERT_FF, D), EXPERT_FF ** -0.5),
    }


def reference(x, mem, rel_bias, mem_norm_g, final_norm_g, hgrn_lb,
              norm_mix_g, norm_xattn_g, norm_ffn_g,
              even_w_in, even_w_out, hgrn_norm_g, conv_dw, conv_db, conv_ln_g, conv_ln_b,
              odd_w_qkv, odd_w_out, lam_q1, lam_k1, lam_q2, lam_k2, subln_g,
              xa_wq, xa_wkv, xa_wo,
              moe_w_grp, moe_b_grp, moe_w_exp, moe_b_exp, moe_w1, moe_w3, moe_w2):
    lb_p = jax.nn.softmax(hgrn_lb.astype(jnp.float32), axis=0)
    lower_bounds = jnp.cumsum(lb_p, axis=0) - lb_p[0]
    mem_n = rms_norm(mem, mem_norm_g)
    for i in range(DEPTH):
        h = rms_norm(x, norm_mix_g[i])
        if i % 2 == 0:
            j = i // 2
            y = hgrn_conv_mixer(h, even_w_in[j], even_w_out[j], lower_bounds[j], hgrn_norm_g[j],
                                conv_dw[j], conv_db[j], conv_ln_g[j], conv_ln_b[j])
        else:
            j = i // 2
            lam_init = 0.8 - 0.6 * math.exp(-0.3 * i)
            y = diff_attention(h, odd_w_qkv[j], odd_w_out[j], lam_q1[j], lam_k1[j], lam_q2[j],
                               lam_k2[j], subln_g[j], rel_bias, lam_init)
        x = x + y
        x = x + memory_cross_attention(rms_norm(x, norm_xattn_g[i]), mem_n, xa_wq[i], xa_wkv[i], xa_wo[i])
        x = x + hierarchical_moe(rms_norm(x, norm_ffn_g[i]), moe_w_grp[i], moe_b_grp[i], moe_w_exp[i],
                                 moe_b_exp[i], moe_w1[i], moe_w3[i], moe_w2[i])
    return rms_norm(x, final_norm_g)
```

```python
import functools
import math

import jax
import jax.numpy as jnp
from jax import lax
from jax.experimental import pallas as pl
from jax.experimental.pallas import tpu as pltpu

F32 = jnp.float32
BF16 = jnp.bfloat16

EPS = 1e-6
NEG_BIG = -1e30
TINY = 1e-30

HGRN_HEADS = 4
HGRN_DK = 128
HGRN_WIDTH = HGRN_HEADS * HGRN_DK
CONV_WIDTH = 31
DA_HEADS = 8
DA_HEAD_DIM = 64
DA_VDIM = 2 * DA_HEAD_DIM
REL_BUCKETS = 32
REL_MAX_DIST = 128
XA_HEADS = 4
N_GROUPS = 4
EXPERTS_PER_GROUP = 4
N_EXPERTS = N_GROUPS * EXPERTS_PER_GROUP

LANES = 128
V7X_VMEM_LIMIT_BYTES = 56 * 1024 * 1024

ROW_TILE = 512
HGRN_CHUNK = 128
HGRN_SUB = 16
HGRN_ROWS = 512
CONV_ROWS = 512
CONV_HALO = 32
ATT_TILE = 512
MOE_ROWS = 1024


def _params(*sem):
    return pltpu.CompilerParams(dimension_semantics=sem, vmem_limit_bytes=V7X_VMEM_LIMIT_BYTES)


def _rms(x, g):
    return x * lax.rsqrt(jnp.mean(x * x, axis=-1, keepdims=True) + EPS) * g


def _dot(a, b):
    return jnp.dot(a, b, preferred_element_type=F32)


def _dot_nt(a, b):
    return lax.dot_general(a, b, (((1,), (1,)), ((), ())), preferred_element_type=F32)


def _norm_matmul_kernel(x_ref, g_ref, w_ref, o_ref):
    h = _rms(x_ref[...], g_ref[...]).astype(BF16)
    o_ref[...] = _dot(h, w_ref[...]).astype(o_ref.dtype)


def _norm_matmul(x, g, w, tm):
    n, d = x.shape
    nout = w.shape[1]
    return pl.pallas_call(
        _norm_matmul_kernel,
        out_shape=jax.ShapeDtypeStruct((n, nout), BF16),
        grid=(n // tm,),
        in_specs=[pl.BlockSpec((tm, d), lambda i: (i, 0)),
                  pl.BlockSpec((1, d), lambda i: (0, 0)),
                  pl.BlockSpec((d, nout), lambda i: (0, 0))],
        out_specs=pl.BlockSpec((tm, nout), lambda i: (i, 0)),
        compiler_params=_params("parallel"),
        name="norm_matmul",
    )(x, g.reshape(1, d), w)


def _resid_matmul_kernel(*refs):
    x_ref, a_refs, w_ref, o_ref = refs[0], refs[1:-2], refs[-2], refs[-1]
    a = [r[...] for r in a_refs]
    a = a[0] if len(a) == 1 else jnp.concatenate(a, axis=1)
    o_ref[...] = x_ref[...] + _dot(a, w_ref[...])


def _resid_matmul(x, acts, w, tm):
    n, d = x.shape
    return pl.pallas_call(
        _resid_matmul_kernel,
        out_shape=jax.ShapeDtypeStruct((n, d), F32),
        grid=(n // tm,),
        in_specs=[pl.BlockSpec((tm, d), lambda i: (i, 0))]
        + [pl.BlockSpec((tm, a.shape[1]), lambda i: (i, 0)) for a in acts]
        + [pl.BlockSpec(w.shape, lambda i: (0, 0))],
        out_specs=pl.BlockSpec((tm, d), lambda i: (i, 0)),
        compiler_params=_params("parallel"),
        name="resid_matmul",
    )(x, *acts, w)


def _cumsum_rows(g):
    rows = lax.broadcasted_iota(jnp.int32, g.shape, 0)
    b, sh = g, 1
    while sh < g.shape[0]:
        b = b + jnp.where(rows >= sh, pltpu.roll(b, sh, 0), 0.0)
        sh *= 2
    return b


def _hgrn_chunk(zq, zf, v, lb, st):
    c = zq.shape[0]
    f = lb + (1.0 - lb) * jax.nn.sigmoid(zf)
    g = jnp.log(jnp.maximum(f, TINY))
    kk = 1.0 - f
    q = zq * jax.nn.sigmoid(zq)
    b = _cumsum_rows(g)

    o = _dot_nt((q * jnp.exp(b)).astype(BF16), st.astype(BF16))
    b_last = b[c - 1:c]
    kdec = (kk * jnp.exp(b_last - b)).astype(BF16)
    st_new = st * jnp.exp(b_last) + _dot(v.T.astype(BF16), kdec)

    rows = lax.broadcasted_iota(jnp.int32, (HGRN_SUB, 1), 0)
    half = HGRN_SUB // 2
    a_rows, d_rows = [], []
    for i in range(c // HGRN_SUB):
        r0 = i * HGRN_SUB
        bi, qi, ki, vi = (t[r0:r0 + HGRN_SUB] for t in (b, q, kk, v))
        if i == 0:
            a_rows.append(jnp.zeros((HGRN_SUB, c), F32))
        else:
            beta = b[r0 - 1:r0]
            qs = (qi * jnp.exp(bi - beta)).astype(BF16)
            ks = (kk[:r0] * jnp.exp(beta - b[:r0])).astype(BF16)
            ks = jnp.concatenate([ks, jnp.zeros((c - r0, ks.shape[1]), BF16)], axis=0)
            a_rows.append(_dot_nt(qs, ks))
        top = jnp.zeros((half, v.shape[1]), F32)
        bot = jnp.zeros((half, v.shape[1]), F32)
        for s in range(HGRN_SUB):
            lo = 0 if s < half else half
            w = qi[lo:] * jnp.exp(jnp.minimum(bi[lo:] - bi[s:s + 1], 0.0)) * ki[s:s + 1]
            rs = jnp.where(rows[lo:] >= s, jnp.sum(w, axis=1, keepdims=True), 0.0)
            contrib = rs * vi[s:s + 1]
            if lo == 0:
                top = top + contrib[:half]
                bot = bot + contrib[half:]
            else:
                bot = bot + contrib
        d_rows += [top, bot]
    a = jnp.concatenate(a_rows, axis=0).astype(BF16)
    o = o + _dot(a, v.astype(BF16)) + jnp.concatenate(d_rows, axis=0)
    return o, st_new


def _hgrn_kernel(zq_ref, zf_ref, zi_ref, zg_ref, lbp_ref, gn_ref, o_ref, st_ref, *, layer):
    @pl.when(pl.program_id(2) == 0)
    def _():
        st_ref[...] = jnp.zeros_like(st_ref)

    lbp = lbp_ref[...]
    e = jnp.exp(lbp - jnp.max(lbp, axis=0, keepdims=True))
    p = e / jnp.sum(e, axis=0, keepdims=True)
    lb = jnp.zeros((1, lbp.shape[1]), F32)
    for r in range(1, layer + 1):
        lb = lb + p[r:r + 1]
    gn = gn_ref[...]

    def chunk(ci, carry):
        r0 = pl.multiple_of(ci * HGRN_CHUNK, HGRN_CHUNK)
        sl = pl.ds(r0, HGRN_CHUNK)
        o, st_new = _hgrn_chunk(zq_ref[sl, :].astype(F32), zf_ref[sl, :].astype(F32),
                                zi_ref[sl, :].astype(F32), lb, st_ref[...])
        st_ref[...] = st_new
        o = _rms(o, gn) * jax.nn.sigmoid(zg_ref[sl, :].astype(F32))
        o_ref[sl, :] = o.astype(o_ref.dtype)
        return carry

    lax.fori_loop(0, o_ref.shape[0] // HGRN_CHUNK, chunk, 0)


def _hgrn(proj, hgrn_lb, gn, layer, batch, seq):
    n = proj.shape[0]
    ns = seq // HGRN_ROWS
    hh = HGRN_HEADS

    def col(k):
        return pl.BlockSpec((HGRN_ROWS, HGRN_DK), lambda b, h, s: (b * ns + s, k * hh + h))

    return pl.pallas_call(
        functools.partial(_hgrn_kernel, layer=layer),
        out_shape=jax.ShapeDtypeStruct((n, HGRN_WIDTH), BF16),
        grid=(batch, hh, ns),
        in_specs=[col(0), col(1), col(2), col(3),
                  pl.BlockSpec((hgrn_lb.shape[0], HGRN_DK), lambda b, h, s: (0, h)),
                  pl.BlockSpec((1, HGRN_DK), lambda b, h, s: (0, 0))],
        out_specs=pl.BlockSpec((HGRN_ROWS, HGRN_DK), lambda b, h, s: (b * ns + s, h)),
        scratch_shapes=[pltpu.VMEM((HGRN_DK, HGRN_DK), F32)],
        compiler_params=_params("parallel", "parallel", "arbitrary"),
        name="hgrn2",
    )(proj, proj, proj, proj, hgrn_lb, gn.reshape(1, HGRN_DK))


def _conv_kernel(cu_ref, cv_ref, dw_ref, db_ref, g_ref, b_ref, o_ref, ext_ref):
    rows = o_ref.shape[0]

    @pl.when(pl.program_id(1) == 0)
    def _():
        ext_ref[0:CONV_HALO, :] = jnp.zeros((CONV_HALO, ext_ref.shape[1]), F32)

    @pl.when(pl.program_id(1) > 0)
    def _():
        ext_ref[0:CONV_HALO, :] = ext_ref[rows:rows + CONV_HALO, :]

    ext_ref[CONV_HALO:, :] = cu_ref[...].astype(F32) * jax.nn.sigmoid(cv_ref[...].astype(F32))
    acc = jnp.zeros(o_ref.shape, F32) + db_ref[...]
    first = CONV_HALO - (CONV_WIDTH - 1)
    for j in range(CONV_WIDTH):
        acc = acc + dw_ref[j:j + 1, :] * ext_ref[first + j:first + j + rows, :]
    mu = jnp.mean(acc, axis=-1, keepdims=True)
    cen = acc - mu
    var = jnp.mean(cen * cen, axis=-1, keepdims=True)
    y = cen * lax.rsqrt(var + EPS) * g_ref[...] + b_ref[...]
    o_ref[...] = (y * jax.nn.sigmoid(y)).astype(o_ref.dtype)


def _conv(proj, dw, db, ln_g, ln_b, batch, seq):
    n = proj.shape[0]
    ch = dw.shape[1]
    ns = seq // CONV_ROWS
    first_blk = 4 * HGRN_WIDTH // ch
    vec = pl.BlockSpec((1, ch), lambda b, s: (0, 0))
    return pl.pallas_call(
        _conv_kernel,
        out_shape=jax.ShapeDtypeStruct((n, ch), BF16),
        grid=(batch, ns),
        in_specs=[pl.BlockSpec((CONV_ROWS, ch), lambda b, s: (b * ns + s, first_blk)),
                  pl.BlockSpec((CONV_ROWS, ch), lambda b, s: (b * ns + s, first_blk + 1)),
                  pl.BlockSpec((CONV_WIDTH, ch), lambda b, s: (0, 0)), vec, vec, vec],
        out_specs=pl.BlockSpec((CONV_ROWS, ch), lambda b, s: (b * ns + s, 0)),
        scratch_shapes=[pltpu.VMEM((CONV_ROWS + CONV_HALO, ch), F32)],
        compiler_params=_params("parallel", "arbitrary"),
        name="glu_conv_ln",
    )(proj, proj, dw, db.reshape(1, ch), ln_g.reshape(1, ch), ln_b.reshape(1, ch))


def _bias_kernel(tab_ref, o_ref):
    h = pl.program_id(0)
    t = o_ref.shape[2]
    k = lax.broadcasted_iota(jnp.int32, (t, t), 0)
    q = lax.broadcasted_iota(jnp.int32, (t, t), 1)
    max_exact = REL_BUCKETS // 2
    for d in range(2):
        rel = q - k + d * t
        nn = jnp.maximum(rel, 0)
        nf = jnp.maximum(nn, 1).astype(F32)
        large = max_exact + (jnp.log(nf / max_exact) / math.log(REL_MAX_DIST / max_exact)
                             * (REL_BUCKETS - max_exact)).astype(jnp.int32)
        large = jnp.clip(large, 0, REL_BUCKETS - 1)
        bucket = jnp.where(nn < max_exact, nn, large)
        val = jnp.zeros((t, t), F32)
        for bk in range(REL_BUCKETS):
            val = jnp.where(bucket == bk, tab_ref[bk, h], val)
        if d == 0:
            val = jnp.where(rel >= 0, val, NEG_BIG)
        o_ref[0, d] = val
    o_ref[0, 2] = jnp.zeros((t, t), F32) + tab_ref[REL_BUCKETS - 1, h]


def _bias_tiles(rel_bias):
    assert ATT_TILE >= REL_MAX_DIST
    heads = rel_bias.shape[1]
    return pl.pallas_call(
        _bias_kernel,
        out_shape=jax.ShapeDtypeStruct((heads, 3, ATT_TILE, ATT_TILE), F32),
        grid=(heads,),
        in_specs=[pl.BlockSpec(memory_space=pltpu.SMEM)],
        out_specs=pl.BlockSpec((1, 3, ATT_TILE, ATT_TILE), lambda h: (h, 0, 0, 0)),
        compiler_params=_params("parallel"),
        name="t5_bias_tiles",
    )(rel_bias)


def _attn_kernel(q_ref, k_ref, v_ref, bias_ref, lamv_ref, g_ref, o_ref, vt_ref, *, lam_init):
    i = pl.program_id(2)
    t = o_ref.shape[0]

    @pl.when(i == 0)
    def _():
        for j in range(vt_ref.shape[0]):
            vt_ref[j] = v_ref[j * t:(j + 1) * t, :].astype(F32).T.astype(BF16)

    lane = lax.broadcasted_iota(jnp.int32, (1, DA_VDIM), 1)
    q = q_ref[...].astype(F32) * (DA_HEAD_DIM ** -0.5)
    qs = (jnp.where(lane < DA_HEAD_DIM, q, 0.0).astype(BF16),
          jnp.where(lane >= DA_HEAD_DIM, q, 0.0).astype(BF16))

    def body(j, carry):
        kb = k_ref[pl.ds(pl.multiple_of(j * t, t), t), :]
        bt = bias_ref[0, jnp.minimum(i - j, 2)]
        vt = vt_ref[j]
        new = []
        for mp in range(2):
            m, l, acc = carry[3 * mp:3 * mp + 3]
            s = _dot_nt(kb, qs[mp]) + bt
            mn = jnp.maximum(m, jnp.max(s, axis=0, keepdims=True))
            p = jnp.exp(s - mn)
            al = jnp.exp(m - mn)
            new += [mn, l * al + jnp.sum(p, axis=0, keepdims=True), acc * al + _dot(vt, p.astype(BF16))]
        return tuple(new)

    stat = (jnp.full((1, t), -jnp.inf, F32), jnp.zeros((1, t), F32), jnp.zeros((DA_VDIM, t), F32))
    _, l1, a1, _, l2, a2 = lax.fori_loop(0, i + 1, body, stat + stat)

    lv = lamv_ref[...]
    lam = (jnp.exp(jnp.sum(lv[0:1] * lv[1:2], axis=1, keepdims=True))
           - jnp.exp(jnp.sum(lv[2:3] * lv[3:4], axis=1, keepdims=True)) + lam_init)
    o = a1 * (1.0 / l1) - lam * (a2 * (1.0 / l2))
    y = o * lax.rsqrt(jnp.mean(o * o, axis=0, keepdims=True) + EPS) * g_ref[...] * (1.0 - lam_init)
    o_ref[...] = y.T.astype(o_ref.dtype)


def _diff_attn(qkv, bias_tiles, lamv, subln_g, lam_init, batch, seq):
    n = qkv.shape[0]
    t = ATT_TILE
    nq = seq // t
    hh = DA_HEADS
    return pl.pallas_call(
        functools.partial(_attn_kernel, lam_init=lam_init),
        out_shape=jax.ShapeDtypeStruct((n, hh * DA_VDIM), BF16),
        grid=(batch, hh, nq),
        in_specs=[pl.BlockSpec((t, DA_VDIM), lambda b, h, i: (b * nq + i, h)),
                  pl.BlockSpec((seq, DA_VDIM), lambda b, h, i: (b, hh + h)),
                  pl.BlockSpec((seq, DA_VDIM), lambda b, h, i: (b, 2 * hh + h)),
                  pl.BlockSpec((1, 3, t, t), lambda b, h, i: (h, 0, 0, 0)),
                  pl.BlockSpec(lamv.shape, lambda b, h, i: (0, 0)),
                  pl.BlockSpec((DA_VDIM, 1), lambda b, h, i: (0, 0))],
        out_specs=pl.BlockSpec((t, DA_VDIM), lambda b, h, i: (b * nq + i, h)),
        scratch_shapes=[pltpu.VMEM((nq, DA_VDIM, t), BF16)],
        compiler_params=_params("parallel", "parallel", "arbitrary"),
        name="diff_attention",
    )(qkv, qkv, qkv, bias_tiles, lamv, subln_g.reshape(DA_VDIM, 1))


def _xattn_kernel(x_ref, g_ref, wq_ref, kv_ref, wo_ref, o_ref):
    x = x_ref[...]
    d = x.shape[1]
    dh = d // XA_HEADS
    h = _rms(x, g_ref[...]).astype(BF16)
    q = (_dot(h, wq_ref[...]) * (dh ** -0.5)).astype(BF16)
    kv = kv_ref[...]
    outs = []
    for hd in range(XA_HEADS):
        s = _dot_nt(q[:, hd * dh:(hd + 1) * dh], kv[:, hd * dh:(hd + 1) * dh])
        p = jnp.exp(s - jnp.max(s, axis=-1, keepdims=True))
        pv = _dot(p.astype(BF16), kv[:, d + hd * dh:d + (hd + 1) * dh])
        outs.append(pv * (1.0 / jnp.sum(p, axis=-1, keepdims=True)))
    o = jnp.concatenate(outs, axis=1).astype(BF16)
    o_ref[...] = x + _dot(o, wo_ref[...])


def _xattn(x, g, wq, kv, wo, seq, mem_len, tm):
    n, d = x.shape
    per_batch = seq // tm
    return pl.pallas_call(
        _xattn_kernel,
        out_shape=jax.ShapeDtypeStruct((n, d), F32),
        grid=(n // tm,),
        in_specs=[pl.BlockSpec((tm, d), lambda i: (i, 0)),
                  pl.BlockSpec((1, d), lambda i: (0, 0)),
                  pl.BlockSpec((d, d), lambda i: (0, 0)),
                  pl.BlockSpec((mem_len, 2 * d), lambda i: (i // per_batch, 0)),
                  pl.BlockSpec((d, d), lambda i: (0, 0))],
        out_specs=pl.BlockSpec((tm, d), lambda i: (i, 0)),
        compiler_params=_params("parallel"),
        name="memory_xattn",
    )(x, g.reshape(1, d), wq, kv, wo)


def _route(logits):
    lane = lax.broadcasted_iota(jnp.int32, logits.shape, 1)
    big = logits.shape[1]
    is_grp = lane < N_GROUPS
    gl = jnp.where(is_grp, logits, -jnp.inf)
    gmax = jnp.max(gl, axis=1, keepdims=True)
    gsel = jnp.min(jnp.where(gl == gmax, lane, big), axis=1, keepdims=True)
    g_w = 1.0 / jnp.sum(jnp.where(is_grp, jnp.exp(logits - gmax), 0.0), axis=1, keepdims=True)
    e_idx = lane - N_GROUPS
    in_grp = (e_idx >= 0) & (e_idx < N_EXPERTS) & (e_idx // EXPERTS_PER_GROUP == gsel)
    ml = jnp.where(in_grp, logits, -jnp.inf)
    v1 = jnp.max(ml, axis=1, keepdims=True)
    i1 = jnp.min(jnp.where(ml == v1, lane, big), axis=1, keepdims=True)
    ml2 = jnp.where(lane == i1, -jnp.inf, ml)
    v2 = jnp.max(ml2, axis=1, keepdims=True)
    i2 = jnp.min(jnp.where(ml2 == v2, lane, big), axis=1, keepdims=True)
    t = jnp.exp(v2 - v1)
    w1 = g_w / (1.0 + t)
    w2 = g_w * t / (1.0 + t)
    return jnp.where(lane == i1, w1, 0.0) + jnp.where(lane == i2, w2, 0.0)


def _moe_kernel(x_ref, g_ref, wr_ref, br_ref, w1_ref, w3_ref, w2_ref, o_ref, h_ref, gate_ref):
    e = pl.program_id(1)

    @pl.when(e == 0)
    def _():
        x = x_ref[...]
        h = _rms(x, g_ref[...])
        h_ref[...] = h.astype(BF16)
        logits = jnp.dot(h, wr_ref[...], preferred_element_type=F32,
                         precision=lax.Precision.HIGHEST) + br_ref[...]
        gate_ref[...] = _route(logits)
        o_ref[...] = x

    h = h_ref[...]
    hid = _dot(h, w1_ref[0])
    hid = hid * jax.nn.sigmoid(hid) * _dot(h, w3_ref[0])
    y = _dot(hid.astype(BF16), w2_ref[0])
    lane = lax.broadcasted_iota(jnp.int32, gate_ref.shape, 1)
    gcol = jnp.sum(jnp.where(lane == e + N_GROUPS, gate_ref[...], 0.0), axis=1, keepdims=True)
    o_ref[...] += gcol * y


def _moe(x, g, wr, br, w1, w3, w2, tm):
    n, d = x.shape
    ff = w1.shape[2]
    return pl.pallas_call(
        _moe_kernel,
        out_shape=jax.ShapeDtypeStruct((n, d), F32),
        grid=(n // tm, N_EXPERTS),
        in_specs=[pl.BlockSpec((tm, d), lambda i, e: (i, 0)),
                  pl.BlockSpec((1, d), lambda i, e: (0, 0)),
                  pl.BlockSpec((d, LANES), lambda i, e: (0, 0)),
                  pl.BlockSpec((1, LANES), lambda i, e: (0, 0)),
                  pl.BlockSpec((1, d, ff), lambda i, e: (e, 0, 0)),
                  pl.BlockSpec((1, d, ff), lambda i, e: (e, 0, 0)),
                  pl.BlockSpec((1, ff, d), lambda i, e: (e, 0, 0))],
        out_specs=pl.BlockSpec((tm, d), lambda i, e: (i, 0)),
        scratch_shapes=[pltpu.VMEM((tm, d), BF16), pltpu.VMEM((tm, LANES), F32)],
        compiler_params=_params("parallel", "arbitrary"),
        name="hier_moe",
    )(x, g.reshape(1, d), wr, br, w1, w3, w2)


def _final_norm_kernel(x_ref, g_ref, o_ref):
    o_ref[...] = _rms(x_ref[...], g_ref[...])


def _final_norm(x, g, tm):
    n, d = x.shape
    return pl.pallas_call(
        _final_norm_kernel,
        out_shape=jax.ShapeDtypeStruct((n, d), F32),
        grid=(n // tm,),
        in_specs=[pl.BlockSpec((tm, d), lambda i: (i, 0)), pl.BlockSpec((1, d), lambda i: (0, 0))],
        out_specs=pl.BlockSpec((tm, d), lambda i: (i, 0)),
        compiler_params=_params("parallel"),
        name="final_norm",
    )(x, g.reshape(1, d))


def kernel(x, mem, rel_bias, mem_norm_g, final_norm_g, hgrn_lb, norm_mix_g, norm_xattn_g, norm_ffn_g, even_w_in, even_w_out, hgrn_norm_g, conv_dw, conv_db, conv_ln_g, conv_ln_b, odd_w_qkv, odd_w_out, lam_q1, lam_k1, lam_q2, lam_k2, subln_g, xa_wq, xa_wkv, xa_wo, moe_w_grp, moe_b_grp, moe_w_exp, moe_b_exp, moe_w1, moe_w3, moe_w2):
    batch, seq, d = x.shape
    mem_len = mem.shape[1]
    depth = norm_mix_g.shape[0]
    n = batch * seq
    assert seq % ATT_TILE == 0 and seq % HGRN_ROWS == 0 and seq % CONV_ROWS == 0 and seq % ROW_TILE == 0
    assert n % MOE_ROWS == 0 and (batch * mem_len) % ROW_TILE == 0

    xf = x.reshape(n, d)
    memf = mem.reshape(batch * mem_len, d)
    bias_tiles = _bias_tiles(rel_bias)
    pad = LANES - N_GROUPS - N_EXPERTS

    for i in range(depth):
        j = i // 2
        if i % 2 == 0:
            proj = _norm_matmul(xf, norm_mix_g[i], even_w_in[j].astype(BF16), ROW_TILE)
            o_hgrn = _hgrn(proj, hgrn_lb, hgrn_norm_g[j], j, batch, seq)
            o_conv = _conv(proj, conv_dw[j], conv_db[j], conv_ln_g[j], conv_ln_b[j], batch, seq)
            xf = _resid_matmul(xf, [o_hgrn, o_conv], even_w_out[j].astype(BF16), ROW_TILE)
        else:
            lam_init = 0.8 - 0.6 * math.exp(-0.3 * i)
            qkv = _norm_matmul(xf, norm_mix_g[i], odd_w_qkv[j].astype(BF16), ROW_TILE)
            lamv = jnp.stack([lam_q1[j], lam_k1[j], lam_q2[j], lam_k2[j]]).astype(F32)
            o_att = _diff_attn(qkv, bias_tiles, lamv, subln_g[j], lam_init, batch, seq)
            xf = _resid_matmul(xf, [o_att], odd_w_out[j].astype(BF16), ROW_TILE)

        kv = _norm_matmul(memf, mem_norm_g, xa_wkv[i].astype(BF16), ROW_TILE)
        xf = _xattn(xf, norm_xattn_g[i], xa_wq[i].astype(BF16), kv, xa_wo[i].astype(BF16), seq, mem_len, ROW_TILE)

        wr = jnp.concatenate([moe_w_grp[i], moe_w_exp[i], jnp.zeros((d, pad), F32)], axis=1)
        br = jnp.concatenate([moe_b_grp[i], moe_b_exp[i], jnp.zeros((pad,), F32)]).reshape(1, LANES)
        xf = _moe(xf, norm_ffn_g[i], wr, br, moe_w1[i].astype(BF16), moe_w3[i].astype(BF16),
                  moe_w2[i].astype(BF16), MOE_ROWS)

    return _final_norm(xf, final_norm_g, ROW_TILE).reshape(batch, seq, d)
```

```python
import functools
import math

import jax
import jax.numpy as jnp
from jax import lax
from jax.experimental import pallas as pl
from jax.experimental.pallas import tpu as pltpu

F32 = jnp.float32
BF16 = jnp.bfloat16

EPS = 1e-6
NEG_BIG = -1e30
TINY = 1e-30

HGRN_HEADS = 4
HGRN_DK = 128
HGRN_WIDTH = HGRN_HEADS * HGRN_DK
CONV_WIDTH = 31
DA_HEADS = 8
DA_HEAD_DIM = 64
DA_VDIM = 2 * DA_HEAD_DIM
REL_BUCKETS = 32
REL_MAX_DIST = 128
XA_HEADS = 4
N_GROUPS = 4
EXPERTS_PER_GROUP = 4
N_EXPERTS = N_GROUPS * EXPERTS_PER_GROUP

LANES = 128
V7X_VMEM_LIMIT_BYTES = 56 * 1024 * 1024

ROW_TILE = 512
HGRN_CHUNK = 128
HGRN_SUB = 16
HGRN_ROWS = 512
CONV_ROWS = 512
CONV_HALO = 32
ATT_TILE = 512
ATT_STRIP = 256
MOE_ROWS = 1024


def _params(*sem):
    return pltpu.CompilerParams(dimension_semantics=sem, vmem_limit_bytes=V7X_VMEM_LIMIT_BYTES)


def _rms(x, g):
    return x * lax.rsqrt(jnp.mean(x * x, axis=-1, keepdims=True) + EPS) * g


def _dot(a, b):
    return jnp.dot(a, b, preferred_element_type=F32)


def _dot_nt(a, b):
    return lax.dot_general(a, b, (((1,), (1,)), ((), ())), preferred_element_type=F32)


def _norm_matmul_kernel(x_ref, g_ref, w_ref, o_ref):
    h = _rms(x_ref[...], g_ref[...]).astype(BF16)
    o_ref[...] = _dot(h, w_ref[...]).astype(o_ref.dtype)


def _norm_matmul(x, g, w, tm):
    n, d = x.shape
    nout = w.shape[1]
    return pl.pallas_call(
        _norm_matmul_kernel,
        out_shape=jax.ShapeDtypeStruct((n, nout), BF16),
        grid=(n // tm,),
        in_specs=[pl.BlockSpec((tm, d), lambda i: (i, 0)),
                  pl.BlockSpec((1, d), lambda i: (0, 0)),
                  pl.BlockSpec((d, nout), lambda i: (0, 0))],
        out_specs=pl.BlockSpec((tm, nout), lambda i: (i, 0)),
        compiler_params=_params("parallel"),
        name="norm_matmul",
    )(x, g.reshape(1, d), w)


def _resid_matmul_kernel(*refs):
    x_ref, a_refs, w_ref, o_ref = refs[0], refs[1:-2], refs[-2], refs[-1]
    a = [r[...] for r in a_refs]
    a = a[0] if len(a) == 1 else jnp.concatenate(a, axis=1)
    o_ref[...] = x_ref[...] + _dot(a, w_ref[...])


def _resid_matmul(x, acts, w, tm):
    n, d = x.shape
    return pl.pallas_call(
        _resid_matmul_kernel,
        out_shape=jax.ShapeDtypeStruct((n, d), F32),
        grid=(n // tm,),
        in_specs=[pl.BlockSpec((tm, d), lambda i: (i, 0))]
        + [pl.BlockSpec((tm, a.shape[1]), lambda i: (i, 0)) for a in acts]
        + [pl.BlockSpec(w.shape, lambda i: (0, 0))],
        out_specs=pl.BlockSpec((tm, d), lambda i: (i, 0)),
        compiler_params=_params("parallel"),
        name="resid_matmul",
    )(x, *acts, w)


def _cumsum_rows(g):
    rows = lax.broadcasted_iota(jnp.int32, g.shape, 0)
    b, sh = g, 1
    while sh < g.shape[0]:
        b = b + jnp.where(rows >= sh, pltpu.roll(b, sh, 0), 0.0)
        sh *= 2
    return b


def _hgrn_chunk(zq, zf, v, lb, st):
    c = zq.shape[0]
    f = lb + (1.0 - lb) * jax.nn.sigmoid(zf)
    g = jnp.log(jnp.maximum(f, TINY))
    kk = 1.0 - f
    q = zq * jax.nn.sigmoid(zq)
    b = _cumsum_rows(g)

    o = _dot_nt((q * jnp.exp(b)).astype(BF16), st.astype(BF16))
    b_last = b[c - 1:c]
    kdec = (kk * jnp.exp(b_last - b)).astype(BF16)
    st_new = st * jnp.exp(b_last) + _dot(v.T.astype(BF16), kdec)

    rows = lax.broadcasted_iota(jnp.int32, (HGRN_SUB, 1), 0)
    half = HGRN_SUB // 2
    a_rows, d_rows = [], []
    for i in range(c // HGRN_SUB):
        r0 = i * HGRN_SUB
        bi, qi, ki, vi = (t[r0:r0 + HGRN_SUB] for t in (b, q, kk, v))
        if i == 0:
            a_rows.append(jnp.zeros((HGRN_SUB, c), F32))
        else:
            beta = b[r0 - 1:r0]
            qs = (qi * jnp.exp(bi - beta)).astype(BF16)
            ks = (kk[:r0] * jnp.exp(beta - b[:r0])).astype(BF16)
            ks = jnp.concatenate([ks, jnp.zeros((c - r0, ks.shape[1]), BF16)], axis=0)
            a_rows.append(_dot_nt(qs, ks))
        top = jnp.zeros((half, v.shape[1]), F32)
        bot = jnp.zeros((half, v.shape[1]), F32)
        for s in range(HGRN_SUB):
            lo = 0 if s < half else half
            w = qi[lo:] * jnp.exp(jnp.minimum(bi[lo:] - bi[s:s + 1], 0.0)) * ki[s:s + 1]
            rs = jnp.where(rows[lo:] >= s, jnp.sum(w, axis=1, keepdims=True), 0.0)
            contrib = rs * vi[s:s + 1]
            if lo == 0:
                top = top + contrib[:half]
                bot = bot + contrib[half:]
            else:
                bot = bot + contrib
        d_rows += [top, bot]
    a = jnp.concatenate(a_rows, axis=0).astype(BF16)
    o = o + _dot(a, v.astype(BF16)) + jnp.concatenate(d_rows, axis=0)
    return o, st_new


def _hgrn_kernel(zq_ref, zf_ref, zi_ref, zg_ref, lbp_ref, gn_ref, o_ref, st_ref, *, layer):
    @pl.when(pl.program_id(2) == 0)
    def _():
        st_ref[...] = jnp.zeros_like(st_ref)

    lbp = lbp_ref[...]
    e = jnp.exp(lbp - jnp.max(lbp, axis=0, keepdims=True))
    p = e / jnp.sum(e, axis=0, keepdims=True)
    lb = jnp.zeros((1, lbp.shape[1]), F32)
    for r in range(1, layer + 1):
        lb = lb + p[r:r + 1]
    gn = gn_ref[...]

    def chunk(ci, carry):
        r0 = pl.multiple_of(ci * HGRN_CHUNK, HGRN_CHUNK)
        sl = pl.ds(r0, HGRN_CHUNK)
        o, st_new = _hgrn_chunk(zq_ref[sl, :].astype(F32), zf_ref[sl, :].astype(F32),
                                zi_ref[sl, :].astype(F32), lb, st_ref[...])
        st_ref[...] = st_new
        o = _rms(o, gn) * jax.nn.sigmoid(zg_ref[sl, :].astype(F32))
        o_ref[sl, :] = o.astype(o_ref.dtype)
        return carry

    lax.fori_loop(0, o_ref.shape[0] // HGRN_CHUNK, chunk, 0)


def _hgrn(proj, hgrn_lb, gn, layer, batch, seq):
    n = proj.shape[0]
    ns = seq // HGRN_ROWS
    hh = HGRN_HEADS

    def col(k):
        return pl.BlockSpec((HGRN_ROWS, HGRN_DK), lambda b, h, s: (b * ns + s, k * hh + h))

    return pl.pallas_call(
        functools.partial(_hgrn_kernel, layer=layer),
        out_shape=jax.ShapeDtypeStruct((n, HGRN_WIDTH), BF16),
        grid=(batch, hh, ns),
        in_specs=[col(0), col(1), col(2), col(3),
                  pl.BlockSpec((hgrn_lb.shape[0], HGRN_DK), lambda b, h, s: (0, h)),
                  pl.BlockSpec((1, HGRN_DK), lambda b, h, s: (0, 0))],
        out_specs=pl.BlockSpec((HGRN_ROWS, HGRN_DK), lambda b, h, s: (b * ns + s, h)),
        scratch_shapes=[pltpu.VMEM((HGRN_DK, HGRN_DK), F32)],
        compiler_params=_params("parallel", "parallel", "arbitrary"),
        name="hgrn2",
    )(proj, proj, proj, proj, hgrn_lb, gn.reshape(1, HGRN_DK))


def _conv_kernel(cu_ref, cv_ref, dw_ref, db_ref, g_ref, b_ref, o_ref, ext_ref):
    rows = o_ref.shape[0]

    @pl.when(pl.program_id(1) == 0)
    def _():
        ext_ref[0:CONV_HALO, :] = jnp.zeros((CONV_HALO, ext_ref.shape[1]), F32)

    @pl.when(pl.program_id(1) > 0)
    def _():
        ext_ref[0:CONV_HALO, :] = ext_ref[rows:rows + CONV_HALO, :]

    ext_ref[CONV_HALO:, :] = cu_ref[...].astype(F32) * jax.nn.sigmoid(cv_ref[...].astype(F32))
    acc = jnp.zeros(o_ref.shape, F32) + db_ref[...]
    first = CONV_HALO - (CONV_WIDTH - 1)
    for j in range(CONV_WIDTH):
        acc = acc + dw_ref[j:j + 1, :] * ext_ref[first + j:first + j + rows, :]
    mu = jnp.mean(acc, axis=-1, keepdims=True)
    cen = acc - mu
    var = jnp.mean(cen * cen, axis=-1, keepdims=True)
    y = cen * lax.rsqrt(var + EPS) * g_ref[...] + b_ref[...]
    o_ref[...] = (y * jax.nn.sigmoid(y)).astype(o_ref.dtype)


def _conv(proj, dw, db, ln_g, ln_b, batch, seq):
    n = proj.shape[0]
    ch = dw.shape[1]
    ns = seq // CONV_ROWS
    first_blk = 4 * HGRN_WIDTH // ch
    vec = pl.BlockSpec((1, ch), lambda b, s: (0, 0))
    return pl.pallas_call(
        _conv_kernel,
        out_shape=jax.ShapeDtypeStruct((n, ch), BF16),
        grid=(batch, ns),
        in_specs=[pl.BlockSpec((CONV_ROWS, ch), lambda b, s: (b * ns + s, first_blk)),
                  pl.BlockSpec((CONV_ROWS, ch), lambda b, s: (b * ns + s, first_blk + 1)),
                  pl.BlockSpec((CONV_WIDTH, ch), lambda b, s: (0, 0)), vec, vec, vec],
        out_specs=pl.BlockSpec((CONV_ROWS, ch), lambda b, s: (b * ns + s, 0)),
        scratch_shapes=[pltpu.VMEM((CONV_ROWS + CONV_HALO, ch), F32)],
        compiler_params=_params("parallel", "arbitrary"),
        name="glu_conv_ln",
    )(proj, proj, dw, db.reshape(1, ch), ln_g.reshape(1, ch), ln_b.reshape(1, ch))


def _bias_kernel(tab_ref, o_ref):
    h = pl.program_id(0)
    t = o_ref.shape[2]
    k = lax.broadcasted_iota(jnp.int32, (t, t), 0)
    q = lax.broadcasted_iota(jnp.int32, (t, t), 1)
    max_exact = REL_BUCKETS // 2
    for d in range(2):
        rel = q - k + d * t
        nn = jnp.maximum(rel, 0)
        nf = jnp.maximum(nn, 1).astype(F32)
        large = max_exact + (jnp.log(nf / max_exact) / math.log(REL_MAX_DIST / max_exact)
                             * (REL_BUCKETS - max_exact)).astype(jnp.int32)
        large = jnp.clip(large, 0, REL_BUCKETS - 1)
        bucket = jnp.where(nn < max_exact, nn, large)
        val = jnp.zeros((t, t), F32)
        for bk in range(REL_BUCKETS):
            val = jnp.where(bucket == bk, tab_ref[bk, h], val)
        if d == 0:
            val = jnp.where(rel >= 0, val, NEG_BIG)
        o_ref[0, d] = val
    o_ref[0, 2] = jnp.zeros((t, t), F32) + tab_ref[REL_BUCKETS - 1, h]


def _bias_tiles(rel_bias):
    assert ATT_TILE >= REL_MAX_DIST
    heads = rel_bias.shape[1]
    return pl.pallas_call(
        _bias_kernel,
        out_shape=jax.ShapeDtypeStruct((heads, 3, ATT_TILE, ATT_TILE), F32),
        grid=(heads,),
        in_specs=[pl.BlockSpec(memory_space=pltpu.SMEM)],
        out_specs=pl.BlockSpec((1, 3, ATT_TILE, ATT_TILE), lambda h: (h, 0, 0, 0)),
        compiler_params=_params("parallel"),
        name="t5_bias_tiles",
    )(rel_bias)


def _attn_kernel(q_ref, k_ref, v_ref, bias_ref, lamv_ref, g_ref, o_ref,
                 vt_ref, qs_ref, m_ref, l_ref, acc_ref, sa_ref, sb_ref, cma_ref, cmb_ref, *, lam_init):
    i = pl.program_id(2)
    t = o_ref.shape[0]

    @pl.when(i == 0)
    def _():
        for j in range(vt_ref.shape[0]):
            vt_ref[j] = v_ref[j * t:(j + 1) * t, :].astype(F32).T.astype(BF16)

    lane = lax.broadcasted_iota(jnp.int32, (1, DA_VDIM), 1)
    q = q_ref[...].astype(F32) * (DA_HEAD_DIM ** -0.5)
    qs_ref[0] = jnp.where(lane < DA_HEAD_DIM, q, 0.0).astype(BF16)
    qs_ref[1] = jnp.where(lane >= DA_HEAD_DIM, q, 0.0).astype(BF16)
    m_ref[...] = jnp.full(m_ref.shape, -jnp.inf, F32)
    l_ref[...] = jnp.zeros(l_ref.shape, F32)
    acc_ref[...] = jnp.zeros(acc_ref.shape, F32)

    def produce(j, s_dst, cm_dst):
        kb = k_ref[pl.ds(pl.multiple_of(j * t, t), t), :]
        bt = bias_ref[0, jnp.minimum(i - j, 2)]
        for mp in range(2):
            s = _dot_nt(kb, qs_ref[mp]) + bt
            s_dst[mp] = s
            cm_dst[mp] = jnp.max(s, axis=0, keepdims=True)

    def consume(j, s_src, cm_src):
        vt = vt_ref[j]
        for mp in range(2):
            for st in range(t // ATT_STRIP):
                cols = slice(st * ATT_STRIP, (st + 1) * ATT_STRIP)
                m_old = m_ref[mp, :, cols]
                mn = jnp.maximum(m_old, cm_src[mp, :, cols])
                p = jnp.exp(s_src[mp, :, cols] - mn)
                al = jnp.exp(m_old - mn)
                m_ref[mp, :, cols] = mn
                l_ref[mp, :, cols] = l_ref[mp, :, cols] * al + jnp.sum(p, axis=0, keepdims=True)
                acc_ref[mp, :, cols] = acc_ref[mp, :, cols] * al + _dot(vt, p.astype(BF16))

    produce(0, sa_ref, cma_ref)

    def pair(pi, carry):
        j = 2 * pi
        produce(j + 1, sb_ref, cmb_ref)
        consume(j, sa_ref, cma_ref)
        produce(j + 2, sa_ref, cma_ref)
        consume(j + 1, sb_ref, cmb_ref)
        return carry

    lax.fori_loop(0, i >> 1, pair, 0)

    @pl.when((i & 1) == 0)
    def _():
        consume(i, sa_ref, cma_ref)

    @pl.when((i & 1) == 1)
    def _():
        produce(i, sb_ref, cmb_ref)
        consume(i - 1, sa_ref, cma_ref)
        consume(i, sb_ref, cmb_ref)

    lv = lamv_ref[...]
    lam = (jnp.exp(jnp.sum(lv[0:1] * lv[1:2], axis=1, keepdims=True))
           - jnp.exp(jnp.sum(lv[2:3] * lv[3:4], axis=1, keepdims=True)) + lam_init)
    o = acc_ref[0] * (1.0 / l_ref[0]) - lam * (acc_ref[1] * (1.0 / l_ref[1]))
    y = o * lax.rsqrt(jnp.mean(o * o, axis=0, keepdims=True) + EPS) * g_ref[...] * (1.0 - lam_init)
    o_ref[...] = y.T.astype(o_ref.dtype)


def _diff_attn(qkv, bias_tiles, lamv, subln_g, lam_init, batch, seq):
    n = qkv.shape[0]
    t = ATT_TILE
    nq = seq // t
    hh = DA_HEADS
    return pl.pallas_call(
        functools.partial(_attn_kernel, lam_init=lam_init),
        out_shape=jax.ShapeDtypeStruct((n, hh * DA_VDIM), BF16),
        grid=(batch, hh, nq),
        in_specs=[pl.BlockSpec((t, DA_VDIM), lambda b, h, i: (b * nq + i, h)),
                  pl.BlockSpec((seq, DA_VDIM), lambda b, h, i: (b, hh + h)),
                  pl.BlockSpec((seq, DA_VDIM), lambda b, h, i: (b, 2 * hh + h)),
                  pl.BlockSpec((1, 3, t, t), lambda b, h, i: (h, 0, 0, 0)),
                  pl.BlockSpec(lamv.shape, lambda b, h, i: (0, 0)),
                  pl.BlockSpec((DA_VDIM, 1), lambda b, h, i: (0, 0))],
        out_specs=pl.BlockSpec((t, DA_VDIM), lambda b, h, i: (b * nq + i, h)),
        scratch_shapes=[pltpu.VMEM((nq, DA_VDIM, t), BF16), pltpu.VMEM((2, t, DA_VDIM), BF16),
                        pltpu.VMEM((2, 1, t), F32), pltpu.VMEM((2, 1, t), F32),
                        pltpu.VMEM((2, DA_VDIM, t), F32),
                        pltpu.VMEM((2, t, t), F32), pltpu.VMEM((2, t, t), F32),
                        pltpu.VMEM((2, 1, t), F32), pltpu.VMEM((2, 1, t), F32)],
        compiler_params=_params("parallel", "parallel", "arbitrary"),
        name="diff_attention",
    )(qkv, qkv, qkv, bias_tiles, lamv, subln_g.reshape(DA_VDIM, 1))


def _xattn_kernel(x_ref, g_ref, wq_ref, kv_ref, wo_ref, o_ref):
    x = x_ref[...]
    d = x.shape[1]
    dh = d // XA_HEADS
    h = _rms(x, g_ref[...]).astype(BF16)
    q = (_dot(h, wq_ref[...]) * (dh ** -0.5)).astype(BF16)
    kv = kv_ref[...]
    outs = []
    for hd in range(XA_HEADS):
        s = _dot_nt(q[:, hd * dh:(hd + 1) * dh], kv[:, hd * dh:(hd + 1) * dh])
        p = jnp.exp(s - jnp.max(s, axis=-1, keepdims=True))
        pv = _dot(p.astype(BF16), kv[:, d + hd * dh:d + (hd + 1) * dh])
        outs.append(pv * (1.0 / jnp.sum(p, axis=-1, keepdims=True)))
    o = jnp.concatenate(outs, axis=1).astype(BF16)
    o_ref[...] = x + _dot(o, wo_ref[...])


def _xattn(x, g, wq, kv, wo, seq, mem_len, tm):
    n, d = x.shape
    per_batch = seq // tm
    return pl.pallas_call(
        _xattn_kernel,
        out_shape=jax.ShapeDtypeStruct((n, d), F32),
        grid=(n // tm,),
        in_specs=[pl.BlockSpec((tm, d), lambda i: (i, 0)),
                  pl.BlockSpec((1, d), lambda i: (0, 0)),
                  pl.BlockSpec((d, d), lambda i: (0, 0)),
                  pl.BlockSpec((mem_len, 2 * d), lambda i: (i // per_batch, 0)),
                  pl.BlockSpec((d, d), lambda i: (0, 0))],
        out_specs=pl.BlockSpec((tm, d), lambda i: (i, 0)),
        compiler_params=_params("parallel"),
        name="memory_xattn",
    )(x, g.reshape(1, d), wq, kv, wo)


def _route(logits):
    lane = lax.broadcasted_iota(jnp.int32, logits.shape, 1)
    big = logits.shape[1]
    is_grp = lane < N_GROUPS
    gl = jnp.where(is_grp, logits, -jnp.inf)
    gmax = jnp.max(gl, axis=1, keepdims=True)
    gsel = jnp.min(jnp.where(gl == gmax, lane, big), axis=1, keepdims=True)
    g_w = 1.0 / jnp.sum(jnp.where(is_grp, jnp.exp(logits - gmax), 0.0), axis=1, keepdims=True)
    e_idx = lane - N_GROUPS
    in_grp = (e_idx >= 0) & (e_idx < N_EXPERTS) & (e_idx // EXPERTS_PER_GROUP == gsel)
    ml = jnp.where(in_grp, logits, -jnp.inf)
    v1 = jnp.max(ml, axis=1, keepdims=True)
    i1 = jnp.min(jnp.where(ml == v1, lane, big), axis=1, keepdims=True)
    ml2 = jnp.where(lane == i1, -jnp.inf, ml)
    v2 = jnp.max(ml2, axis=1, keepdims=True)
    i2 = jnp.min(jnp.where(ml2 == v2, lane, big), axis=1, keepdims=True)
    t = jnp.exp(v2 - v1)
    w1 = g_w / (1.0 + t)
    w2 = g_w * t / (1.0 + t)
    return jnp.where(lane == i1, w1, 0.0) + jnp.where(lane == i2, w2, 0.0)


def _moe_kernel(x_ref, g_ref, wr_ref, br_ref, w1_ref, w3_ref, w2_ref, o_ref, h_ref, gate_ref):
    e = pl.program_id(1)

    @pl.when(e == 0)
    def _():
        x = x_ref[...]
        h = _rms(x, g_ref[...])
        h_ref[...] = h.astype(BF16)
        logits = jnp.dot(h, wr_ref[...], preferred_element_type=F32,
                         precision=lax.Precision.HIGHEST) + br_ref[...]
        gate_ref[...] = _route(logits)
        o_ref[...] = x

    h = h_ref[...]
    hid = _dot(h, w1_ref[0])
    hid = hid * jax.nn.sigmoid(hid) * _dot(h, w3_ref[0])
    y = _dot(hid.astype(BF16), w2_ref[0])
    lane = lax.broadcasted_iota(jnp.int32, gate_ref.shape, 1)
    gcol = jnp.sum(jnp.where(lane == e + N_GROUPS, gate_ref[...], 0.0), axis=1, keepdims=True)
    o_ref[...] += gcol * y


def _moe(x, g, wr, br, w1, w3, w2, tm):
    n, d = x.shape
    ff = w1.shape[2]
    return pl.pallas_call(
        _moe_kernel,
        out_shape=jax.ShapeDtypeStruct((n, d), F32),
        grid=(n // tm, N_EXPERTS),
        in_specs=[pl.BlockSpec((tm, d), lambda i, e: (i, 0)),
                  pl.BlockSpec((1, d), lambda i, e: (0, 0)),
                  pl.BlockSpec((d, LANES), lambda i, e: (0, 0)),
                  pl.BlockSpec((1, LANES), lambda i, e: (0, 0)),
                  pl.BlockSpec((1, d, ff), lambda i, e: (e, 0, 0)),
                  pl.BlockSpec((1, d, ff), lambda i, e: (e, 0, 0)),
                  pl.BlockSpec((1, ff, d), lambda i, e: (e, 0, 0))],
        out_specs=pl.BlockSpec((tm, d), lambda i, e: (i, 0)),
        scratch_shapes=[pltpu.VMEM((tm, d), BF16), pltpu.VMEM((tm, LANES), F32)],
        compiler_params=_params("parallel", "arbitrary"),
        name="hier_moe",
    )(x, g.reshape(1, d), wr, br, w1, w3, w2)


def _final_norm_kernel(x_ref, g_ref, o_ref):
    o_ref[...] = _rms(x_ref[...], g_ref[...])


def _final_norm(x, g, tm):
    n, d = x.shape
    return pl.pallas_call(
        _final_norm_kernel,
        out_shape=jax.ShapeDtypeStruct((n, d), F32),
        grid=(n // tm,),
        in_specs=[pl.BlockSpec((tm, d), lambda i: (i, 0)), pl.BlockSpec((1, d), lambda i: (0, 0))],
        out_specs=pl.BlockSpec((tm, d), lambda i: (i, 0)),
        compiler_params=_params("parallel"),
        name="final_norm",
    )(x, g.reshape(1, d))


def kernel(x, mem, rel_bias, mem_norm_g, final_norm_g, hgrn_lb, norm_mix_g, norm_xattn_g, norm_ffn_g, even_w_in, even_w_out, hgrn_norm_g, conv_dw, conv_db, conv_ln_g, conv_ln_b, odd_w_qkv, odd_w_out, lam_q1, lam_k1, lam_q2, lam_k2, subln_g, xa_wq, xa_wkv, xa_wo, moe_w_grp, moe_b_grp, moe_w_exp, moe_b_exp, moe_w1, moe_w3, moe_w2):
    batch, seq, d = x.shape
    mem_len = mem.shape[1]
    depth = norm_mix_g.shape[0]
    n = batch * seq
    assert seq % ATT_TILE == 0 and seq % HGRN_ROWS == 0 and seq % CONV_ROWS == 0 and seq % ROW_TILE == 0
    assert n % MOE_ROWS == 0 and (batch * mem_len) % ROW_TILE == 0

    xf = x.reshape(n, d)
    memf = mem.reshape(batch * mem_len, d)
    bias_tiles = _bias_tiles(rel_bias)
    pad = LANES - N_GROUPS - N_EXPERTS

    for i in range(depth):
        j = i // 2
        if i % 2 == 0:
            proj = _norm_matmul(xf, norm_mix_g[i], even_w_in[j].astype(BF16), ROW_TILE)
            o_hgrn = _hgrn(proj, hgrn_lb, hgrn_norm_g[j], j, batch, seq)
            o_conv = _conv(proj, conv_dw[j], conv_db[j], conv_ln_g[j], conv_ln_b[j], batch, seq)
            xf = _resid_matmul(xf, [o_hgrn, o_conv], even_w_out[j].astype(BF16), ROW_TILE)
        else:
            lam_init = 0.8 - 0.6 * math.exp(-0.3 * i)
            qkv = _norm_matmul(xf, norm_mix_g[i], odd_w_qkv[j].astype(BF16), ROW_TILE)
            lamv = jnp.stack([lam_q1[j], lam_k1[j], lam_q2[j], lam_k2[j]]).astype(F32)
            o_att = _diff_attn(qkv, bias_tiles, lamv, subln_g[j], lam_init, batch, seq)
            xf = _resid_matmul(xf, [o_att], odd_w_out[j].astype(BF16), ROW_TILE)

        kv = _norm_matmul(memf, mem_norm_g, xa_wkv[i].astype(BF16), ROW_TILE)
        xf = _xattn(xf, norm_xattn_g[i], xa_wq[i].astype(BF16), kv, xa_wo[i].astype(BF16), seq, mem_len, ROW_TILE)

        wr = jnp.concatenate([moe_w_grp[i], moe_w_exp[i], jnp.zeros((d, pad), F32)], axis=1)
        br = jnp.concatenate([moe_b_grp[i], moe_b_exp[i], jnp.zeros((pad,), F32)]).reshape(1, LANES)
        xf = _moe(xf, norm_ffn_g[i], wr, br, moe_w1[i].astype(BF16), moe_w3[i].astype(BF16),
                  moe_w2[i].astype(BF16), MOE_ROWS)

    return _final_norm(xf, final_norm_g, ROW_TILE).reshape(batch, seq, d)
```

```python
import functools
import math

import jax
import jax.numpy as jnp
from jax import lax
from jax.experimental import pallas as pl
from jax.experimental.pallas import tpu as pltpu

F32 = jnp.float32
BF16 = jnp.bfloat16

EPS = 1e-6
NEG_BIG = -1e30
TINY = 1e-30

HGRN_HEADS = 4
HGRN_DK = 128
HGRN_WIDTH = HGRN_HEADS * HGRN_DK
CONV_WIDTH = 31
DA_HEADS = 8
DA_HEAD_DIM = 64
DA_VDIM = 2 * DA_HEAD_DIM
REL_BUCKETS = 32
REL_MAX_DIST = 128
XA_HEADS = 4
N_GROUPS = 4
EXPERTS_PER_GROUP = 4
N_EXPERTS = N_GROUPS * EXPERTS_PER_GROUP

LANES = 128
TOKEN_SUBLANES = 8
V7X_VMEM_LIMIT_BYTES = 56 * 1024 * 1024

ROW_TILE = 512
HGRN_CHUNK = 128
HGRN_SUB = 16
HGRN_ROWS = 512
CONV_ROWS = 512
CONV_HALO = 32
ATT_TILE = 512
ATT_STRIP = 256
MOE_ROWS = 1024
ROUTE_ROWS = 24
MOE_BLOCK = 1024
PERMUTE_ROWS = 1024
PERMUTE_UNROLL = 8


def _params(*sem):
    return pltpu.CompilerParams(dimension_semantics=sem, vmem_limit_bytes=V7X_VMEM_LIMIT_BYTES)


def _rms(x, g):
    return x * lax.rsqrt(jnp.mean(x * x, axis=-1, keepdims=True) + EPS) * g


def _dot(a, b):
    return jnp.dot(a, b, preferred_element_type=F32)


def _dot_nt(a, b):
    return lax.dot_general(a, b, (((1,), (1,)), ((), ())), preferred_element_type=F32)


def _token_rows(s, n):
    return (pl.ds(s, n, stride=TOKEN_SUBLANES), slice(None))


def _from_token_tiles(y_ref):
    n = y_ref.shape[0] // TOKEN_SUBLANES
    return jnp.concatenate([y_ref[_token_rows(s, n)] for s in range(TOKEN_SUBLANES)], axis=1)


def _norm_matmul_kernel(x_ref, g_ref, w_ref, o_ref):
    h = _rms(x_ref[...], g_ref[...]).astype(BF16)
    o_ref[...] = _dot(h, w_ref[...]).astype(o_ref.dtype)


def _norm_matmul(x, g, w, tm):
    n, d = x.shape
    nout = w.shape[1]
    return pl.pallas_call(
        _norm_matmul_kernel,
        out_shape=jax.ShapeDtypeStruct((n, nout), BF16),
        grid=(n // tm,),
        in_specs=[pl.BlockSpec((tm, d), lambda i: (i, 0)),
                  pl.BlockSpec((1, d), lambda i: (0, 0)),
                  pl.BlockSpec((d, nout), lambda i: (0, 0))],
        out_specs=pl.BlockSpec((tm, nout), lambda i: (i, 0)),
        compiler_params=_params("parallel"),
        name="norm_matmul",
    )(x, g.reshape(1, d), w)


def _add_norm_matmul_kernel(x_ref, y_ref, g_ref, w_ref, xo_ref, o_ref):
    x = x_ref[...] + _from_token_tiles(y_ref)
    xo_ref[...] = x
    o_ref[...] = _dot(_rms(x, g_ref[...]).astype(BF16), w_ref[...]).astype(o_ref.dtype)


def _add_norm_matmul(x, y, g, w, tm):
    n, d = x.shape
    nout = w.shape[1]
    row = pl.BlockSpec((tm, d), lambda i: (i, 0))
    tiles = pl.BlockSpec((tm * TOKEN_SUBLANES, LANES), lambda i: (i, 0))
    return pl.pallas_call(
        _add_norm_matmul_kernel,
        out_shape=(jax.ShapeDtypeStruct((n, d), F32), jax.ShapeDtypeStruct((n, nout), BF16)),
        grid=(n // tm,),
        in_specs=[row, tiles, pl.BlockSpec((1, d), lambda i: (0, 0)), pl.BlockSpec((d, nout), lambda i: (0, 0))],
        out_specs=(row, pl.BlockSpec((tm, nout), lambda i: (i, 0))),
        compiler_params=_params("parallel"),
        name="add_norm_matmul",
    )(x, y, g.reshape(1, d), w)


def _resid_matmul_kernel(*refs):
    x_ref, a_refs, w_ref, o_ref = refs[0], refs[1:-2], refs[-2], refs[-1]
    a = [r[...] for r in a_refs]
    a = a[0] if len(a) == 1 else jnp.concatenate(a, axis=1)
    o_ref[...] = x_ref[...] + _dot(a, w_ref[...])


def _resid_matmul(x, acts, w, tm):
    n, d = x.shape
    return pl.pallas_call(
        _resid_matmul_kernel,
        out_shape=jax.ShapeDtypeStruct((n, d), F32),
        grid=(n // tm,),
        in_specs=[pl.BlockSpec((tm, d), lambda i: (i, 0))]
        + [pl.BlockSpec((tm, a.shape[1]), lambda i: (i, 0)) for a in acts]
        + [pl.BlockSpec(w.shape, lambda i: (0, 0))],
        out_specs=pl.BlockSpec((tm, d), lambda i: (i, 0)),
        compiler_params=_params("parallel"),
        name="resid_matmul",
    )(x, *acts, w)


def _cumsum_rows(g):
    rows = lax.broadcasted_iota(jnp.int32, g.shape, 0)
    b, sh = g, 1
    while sh < g.shape[0]:
        b = b + jnp.where(rows >= sh, pltpu.roll(b, sh, 0), 0.0)
        sh *= 2
    return b


def _hgrn_chunk(zq, zf, v, lb, st):
    c = zq.shape[0]
    f = lb + (1.0 - lb) * jax.nn.sigmoid(zf)
    g = jnp.log(jnp.maximum(f, TINY))
    kk = 1.0 - f
    q = zq * jax.nn.sigmoid(zq)
    b = _cumsum_rows(g)

    o = _dot_nt((q * jnp.exp(b)).astype(BF16), st.astype(BF16))
    b_last = b[c - 1:c]
    kdec = (kk * jnp.exp(b_last - b)).astype(BF16)
    st_new = st * jnp.exp(b_last) + _dot(v.T.astype(BF16), kdec)

    rows = lax.broadcasted_iota(jnp.int32, (HGRN_SUB, 1), 0)
    half = HGRN_SUB // 2
    a_rows, d_rows = [], []
    for i in range(c // HGRN_SUB):
        r0 = i * HGRN_SUB
        bi, qi, ki, vi = (t[r0:r0 + HGRN_SUB] for t in (b, q, kk, v))
        if i == 0:
            a_rows.append(jnp.zeros((HGRN_SUB, c), F32))
        else:
            beta = b[r0 - 1:r0]
            qs = (qi * jnp.exp(bi - beta)).astype(BF16)
            ks = (kk[:r0] * jnp.exp(beta - b[:r0])).astype(BF16)
            ks = jnp.concatenate([ks, jnp.zeros((c - r0, ks.shape[1]), BF16)], axis=0)
            a_rows.append(_dot_nt(qs, ks))
        top = jnp.zeros((half, v.shape[1]), F32)
        bot = jnp.zeros((half, v.shape[1]), F32)
        for s in range(HGRN_SUB):
            lo = 0 if s < half else half
            w = qi[lo:] * jnp.exp(jnp.minimum(bi[lo:] - bi[s:s + 1], 0.0)) * ki[s:s + 1]
            rs = jnp.where(rows[lo:] >= s, jnp.sum(w, axis=1, keepdims=True), 0.0)
            contrib = rs * vi[s:s + 1]
            if lo == 0:
                top = top + contrib[:half]
                bot = bot + contrib[half:]
            else:
                bot = bot + contrib
        d_rows += [top, bot]
    a = jnp.concatenate(a_rows, axis=0).astype(BF16)
    o = o + _dot(a, v.astype(BF16)) + jnp.concatenate(d_rows, axis=0)
    return o, st_new


def _hgrn_kernel(zq_ref, zf_ref, zi_ref, zg_ref, lbp_ref, gn_ref, o_ref, st_ref, *, layer):
    @pl.when(pl.program_id(2) == 0)
    def _():
        st_ref[...] = jnp.zeros_like(st_ref)

    lbp = lbp_ref[...]
    e = jnp.exp(lbp - jnp.max(lbp, axis=0, keepdims=True))
    p = e / jnp.sum(e, axis=0, keepdims=True)
    lb = jnp.zeros((1, lbp.shape[1]), F32)
    for r in range(1, layer + 1):
        lb = lb + p[r:r + 1]
    gn = gn_ref[...]

    def chunk(ci, carry):
        r0 = pl.multiple_of(ci * HGRN_CHUNK, HGRN_CHUNK)
        sl = pl.ds(r0, HGRN_CHUNK)
        o, st_new = _hgrn_chunk(zq_ref[sl, :].astype(F32), zf_ref[sl, :].astype(F32),
                                zi_ref[sl, :].astype(F32), lb, st_ref[...])
        st_ref[...] = st_new
        o = _rms(o, gn) * jax.nn.sigmoid(zg_ref[sl, :].astype(F32))
        o_ref[sl, :] = o.astype(o_ref.dtype)
        return carry

    lax.fori_loop(0, o_ref.shape[0] // HGRN_CHUNK, chunk, 0)


def _hgrn(proj, hgrn_lb, gn, layer, batch, seq):
    n = proj.shape[0]
    ns = seq // HGRN_ROWS
    hh = HGRN_HEADS

    def col(k):
        return pl.BlockSpec((HGRN_ROWS, HGRN_DK), lambda b, h, s: (b * ns + s, k * hh + h))

    return pl.pallas_call(
        functools.partial(_hgrn_kernel, layer=layer),
        out_shape=jax.ShapeDtypeStruct((n, HGRN_WIDTH), BF16),
        grid=(batch, hh, ns),
        in_specs=[col(0), col(1), col(2), col(3),
                  pl.BlockSpec((hgrn_lb.shape[0], HGRN_DK), lambda b, h, s: (0, h)),
                  pl.BlockSpec((1, HGRN_DK), lambda b, h, s: (0, 0))],
        out_specs=pl.BlockSpec((HGRN_ROWS, HGRN_DK), lambda b, h, s: (b * ns + s, h)),
        scratch_shapes=[pltpu.VMEM((HGRN_DK, HGRN_DK), F32)],
        compiler_params=_params("parallel", "parallel", "arbitrary"),
        name="hgrn2",
    )(proj, proj, proj, proj, hgrn_lb, gn.reshape(1, HGRN_DK))


def _conv_kernel(cu_ref, cv_ref, dw_ref, db_ref, g_ref, b_ref, o_ref, ext_ref):
    rows = o_ref.shape[0]

    @pl.when(pl.program_id(1) == 0)
    def _():
        ext_ref[0:CONV_HALO, :] = jnp.zeros((CONV_HALO, ext_ref.shape[1]), F32)

    @pl.when(pl.program_id(1) > 0)
    def _():
        ext_ref[0:CONV_HALO, :] = ext_ref[rows:rows + CONV_HALO, :]

    ext_ref[CONV_HALO:, :] = cu_ref[...].astype(F32) * jax.nn.sigmoid(cv_ref[...].astype(F32))
    acc = jnp.zeros(o_ref.shape, F32) + db_ref[...]
    first = CONV_HALO - (CONV_WIDTH - 1)
    for j in range(CONV_WIDTH):
        acc = acc + dw_ref[j:j + 1, :] * ext_ref[first + j:first + j + rows, :]
    mu = jnp.mean(acc, axis=-1, keepdims=True)
    cen = acc - mu
    var = jnp.mean(cen * cen, axis=-1, keepdims=True)
    y = cen * lax.rsqrt(var + EPS) * g_ref[...] + b_ref[...]
    o_ref[...] = (y * jax.nn.sigmoid(y)).astype(o_ref.dtype)


def _conv(proj, dw, db, ln_g, ln_b, batch, seq):
    n = proj.shape[0]
    ch = dw.shape[1]
    ns = seq // CONV_ROWS
    first_blk = 4 * HGRN_WIDTH // ch
    vec = pl.BlockSpec((1, ch), lambda b, s: (0, 0))
    return pl.pallas_call(
        _conv_kernel,
        out_shape=jax.ShapeDtypeStruct((n, ch), BF16),
        grid=(batch, ns),
        in_specs=[pl.BlockSpec((CONV_ROWS, ch), lambda b, s: (b * ns + s, first_blk)),
                  pl.BlockSpec((CONV_ROWS, ch), lambda b, s: (b * ns + s, first_blk + 1)),
                  pl.BlockSpec((CONV_WIDTH, ch), lambda b, s: (0, 0)), vec, vec, vec],
        out_specs=pl.BlockSpec((CONV_ROWS, ch), lambda b, s: (b * ns + s, 0)),
        scratch_shapes=[pltpu.VMEM((CONV_ROWS + CONV_HALO, ch), F32)],
        compiler_params=_params("parallel", "arbitrary"),
        name="glu_conv_ln",
    )(proj, proj, dw, db.reshape(1, ch), ln_g.reshape(1, ch), ln_b.reshape(1, ch))


def _bias_kernel(tab_ref, o_ref):
    h = pl.program_id(0)
    t = o_ref.shape[2]
    k = lax.broadcasted_iota(jnp.int32, (t, t), 0)
    q = lax.broadcasted_iota(jnp.int32, (t, t), 1)
    max_exact = REL_BUCKETS // 2
    for d in range(2):
        rel = q - k + d * t
        nn = jnp.maximum(rel, 0)
        nf = jnp.maximum(nn, 1).astype(F32)
        large = max_exact + (jnp.log(nf / max_exact) / math.log(REL_MAX_DIST / max_exact)
                             * (REL_BUCKETS - max_exact)).astype(jnp.int32)
        large = jnp.clip(large, 0, REL_BUCKETS - 1)
        bucket = jnp.where(nn < max_exact, nn, large)
        val = jnp.zeros((t, t), F32)
        for bk in range(REL_BUCKETS):
            val = jnp.where(bucket == bk, tab_ref[bk, h], val)
        if d == 0:
            val = jnp.where(rel >= 0, val, NEG_BIG)
        o_ref[0, d] = val
    o_ref[0, 2] = jnp.zeros((t, t), F32) + tab_ref[REL_BUCKETS - 1, h]


def _bias_tiles(rel_bias):
    assert ATT_TILE >= REL_MAX_DIST
    heads = rel_bias.shape[1]
    return pl.pallas_call(
        _bias_kernel,
        out_shape=jax.ShapeDtypeStruct((heads, 3, ATT_TILE, ATT_TILE), F32),
        grid=(heads,),
        in_specs=[pl.BlockSpec(memory_space=pltpu.SMEM)],
        out_specs=pl.BlockSpec((1, 3, ATT_TILE, ATT_TILE), lambda h: (h, 0, 0, 0)),
        compiler_params=_params("parallel"),
        name="t5_bias_tiles",
    )(rel_bias)


def _attn_kernel(q_ref, k_ref, v_ref, bias_ref, lamv_ref, g_ref, o_ref,
                 vt_ref, qs_ref, m_ref, l_ref, acc_ref, sa_ref, sb_ref, cma_ref, cmb_ref, *, lam_init):
    i = pl.program_id(2)
    t = o_ref.shape[0]

    @pl.when(i == 0)
    def _():
        for j in range(vt_ref.shape[0]):
            vt_ref[j] = v_ref[j * t:(j + 1) * t, :].astype(F32).T.astype(BF16)

    lane = lax.broadcasted_iota(jnp.int32, (1, DA_VDIM), 1)
    q = q_ref[...].astype(F32) * (DA_HEAD_DIM ** -0.5)
    qs_ref[0] = jnp.where(lane < DA_HEAD_DIM, q, 0.0).astype(BF16)
    qs_ref[1] = jnp.where(lane >= DA_HEAD_DIM, q, 0.0).astype(BF16)
    m_ref[...] = jnp.full(m_ref.shape, -jnp.inf, F32)
    l_ref[...] = jnp.zeros(l_ref.shape, F32)
    acc_ref[...] = jnp.zeros(acc_ref.shape, F32)

    def produce(j, s_dst, cm_dst):
        kb = k_ref[pl.ds(pl.multiple_of(j * t, t), t), :]
        bt = bias_ref[0, jnp.minimum(i - j, 2)]
        for mp in range(2):
            s = _dot_nt(kb, qs_ref[mp]) + bt
            s_dst[mp] = s
            cm_dst[mp] = jnp.max(s, axis=0, keepdims=True)

    def consume(j, s_src, cm_src):
        vt = vt_ref[j]
        for mp in range(2):
            for st in range(t // ATT_STRIP):
                cols = slice(st * ATT_STRIP, (st + 1) * ATT_STRIP)
                m_old = m_ref[mp, :, cols]
                mn = jnp.maximum(m_old, cm_src[mp, :, cols])
                p = jnp.exp(s_src[mp, :, cols] - mn)
                al = jnp.exp(m_old - mn)
                m_ref[mp, :, cols] = mn
                l_ref[mp, :, cols] = l_ref[mp, :, cols] * al + jnp.sum(p, axis=0, keepdims=True)
                acc_ref[mp, :, cols] = acc_ref[mp, :, cols] * al + _dot(vt, p.astype(BF16))

    produce(0, sa_ref, cma_ref)

    def pair(pi, carry):
        j = 2 * pi
        produce(j + 1, sb_ref, cmb_ref)
        consume(j, sa_ref, cma_ref)
        produce(j + 2, sa_ref, cma_ref)
        consume(j + 1, sb_ref, cmb_ref)
        return carry

    lax.fori_loop(0, i >> 1, pair, 0)

    @pl.when((i & 1) == 0)
    def _():
        consume(i, sa_ref, cma_ref)

    @pl.when((i & 1) == 1)
    def _():
        produce(i, sb_ref, cmb_ref)
        consume(i - 1, sa_ref, cma_ref)
        consume(i, sb_ref, cmb_ref)

    lv = lamv_ref[...]
    lam = (jnp.exp(jnp.sum(lv[0:1] * lv[1:2], axis=1, keepdims=True))
           - jnp.exp(jnp.sum(lv[2:3] * lv[3:4], axis=1, keepdims=True)) + lam_init)
    o = acc_ref[0] * (1.0 / l_ref[0]) - lam * (acc_ref[1] * (1.0 / l_ref[1]))
    y = o * lax.rsqrt(jnp.mean(o * o, axis=0, keepdims=True) + EPS) * g_ref[...] * (1.0 - lam_init)
    o_ref[...] = y.T.astype(o_ref.dtype)


def _diff_attn(qkv, bias_tiles, lamv, subln_g, lam_init, batch, seq):
    n = qkv.shape[0]
    t = ATT_TILE
    nq = seq // t
    hh = DA_HEADS
    return pl.pallas_call(
        functools.partial(_attn_kernel, lam_init=lam_init),
        out_shape=jax.ShapeDtypeStruct((n, hh * DA_VDIM), BF16),
        grid=(batch, hh, nq),
        in_specs=[pl.BlockSpec((t, DA_VDIM), lambda b, h, i: (b * nq + i, h)),
                  pl.BlockSpec((seq, DA_VDIM), lambda b, h, i: (b, hh + h)),
                  pl.BlockSpec((seq, DA_VDIM), lambda b, h, i: (b, 2 * hh + h)),
                  pl.BlockSpec((1, 3, t, t), lambda b, h, i: (h, 0, 0, 0)),
                  pl.BlockSpec(lamv.shape, lambda b, h, i: (0, 0)),
                  pl.BlockSpec((DA_VDIM, 1), lambda b, h, i: (0, 0))],
        out_specs=pl.BlockSpec((t, DA_VDIM), lambda b, h, i: (b * nq + i, h)),
        scratch_shapes=[pltpu.VMEM((nq, DA_VDIM, t), BF16), pltpu.VMEM((2, t, DA_VDIM), BF16),
                        pltpu.VMEM((2, 1, t), F32), pltpu.VMEM((2, 1, t), F32),
                        pltpu.VMEM((2, DA_VDIM, t), F32),
                        pltpu.VMEM((2, t, t), F32), pltpu.VMEM((2, t, t), F32),
                        pltpu.VMEM((2, 1, t), F32), pltpu.VMEM((2, 1, t), F32)],
        compiler_params=_params("parallel", "parallel", "arbitrary"),
        name="diff_attention",
    )(qkv, qkv, qkv, bias_tiles, lamv, subln_g.reshape(DA_VDIM, 1))


def _xattn_kernel(x_ref, g_ref, wq_ref, kv_ref, wo_ref, o_ref):
    x = x_ref[...]
    d = x.shape[1]
    dh = d // XA_HEADS
    h = _rms(x, g_ref[...]).astype(BF16)
    q = (_dot(h, wq_ref[...]) * (dh ** -0.5)).astype(BF16)
    kv = kv_ref[...]
    outs = []
    for hd in range(XA_HEADS):
        s = _dot_nt(q[:, hd * dh:(hd + 1) * dh], kv[:, hd * dh:(hd + 1) * dh])
        p = jnp.exp(s - jnp.max(s, axis=-1, keepdims=True))
        pv = _dot(p.astype(BF16), kv[:, d + hd * dh:d + (hd + 1) * dh])
        outs.append(pv * (1.0 / jnp.sum(p, axis=-1, keepdims=True)))
    o = jnp.concatenate(outs, axis=1).astype(BF16)
    o_ref[...] = x + _dot(o, wo_ref[...])


def _xattn(x, g, wq, kv, wo, seq, mem_len, tm):
    n, d = x.shape
    per_batch = seq // tm
    return pl.pallas_call(
        _xattn_kernel,
        out_shape=jax.ShapeDtypeStruct((n, d), F32),
        grid=(n // tm,),
        in_specs=[pl.BlockSpec((tm, d), lambda i: (i, 0)),
                  pl.BlockSpec((1, d), lambda i: (0, 0)),
                  pl.BlockSpec((d, d), lambda i: (0, 0)),
                  pl.BlockSpec((mem_len, 2 * d), lambda i: (i // per_batch, 0)),
                  pl.BlockSpec((d, d), lambda i: (0, 0))],
        out_specs=pl.BlockSpec((tm, d), lambda i: (i, 0)),
        compiler_params=_params("parallel"),
        name="memory_xattn",
    )(x, g.reshape(1, d), wq, kv, wo)


def _route_t(lg):
    row = lax.broadcasted_iota(jnp.int32, lg.shape, 0)
    big = lg.shape[0]
    is_grp = row < N_GROUPS
    gl = jnp.where(is_grp, lg, -jnp.inf)
    gmax = jnp.max(gl, axis=0, keepdims=True)
    gsel = jnp.min(jnp.where(gl == gmax, row, big), axis=0, keepdims=True)
    g_w = 1.0 / jnp.sum(jnp.where(is_grp, jnp.exp(lg - gmax), 0.0), axis=0, keepdims=True)
    e_idx = row - N_GROUPS
    in_grp = (e_idx >= 0) & (e_idx < N_EXPERTS) & (e_idx // EXPERTS_PER_GROUP == gsel)
    ml = jnp.where(in_grp, lg, -jnp.inf)
    v1 = jnp.max(ml, axis=0, keepdims=True)
    i1 = jnp.min(jnp.where(ml == v1, row, big), axis=0, keepdims=True)
    ml2 = jnp.where(row == i1, -jnp.inf, ml)
    v2 = jnp.max(ml2, axis=0, keepdims=True)
    i2 = jnp.min(jnp.where(ml2 == v2, row, big), axis=0, keepdims=True)
    t = jnp.exp(v2 - v1)
    w1 = g_w / (1.0 + t)
    w2 = g_w * t / (1.0 + t)
    return gsel, jnp.where(row == i1, w1, 0.0) + jnp.where(row == i2, w2, 0.0)


def _router_kernel(x_ref, g_ref, wrt_ref, brt_ref, tok_ref, idx_ref, cnt_ref, tri_ref, run_ref):
    t, d = x_ref.shape

    @pl.when(pl.program_id(0) == 0)
    def _():
        r = lax.broadcasted_iota(jnp.int32, (t, t), 0)
        c = lax.broadcasted_iota(jnp.int32, (t, t), 1)
        tri_ref[...] = (r < c).astype(BF16)
        run_ref[...] = jnp.zeros_like(run_ref)

    h = _rms(x_ref[...], g_ref[...])
    logits = lax.dot_general(wrt_ref[...], h, (((1,), (1,)), ((), ())), preferred_element_type=F32,
                             precision=lax.Precision.HIGHEST) + brt_ref[...]
    gsel, gate = _route_t(logits[0:ROUTE_ROWS])

    row8 = lax.broadcasted_iota(jnp.int32, (8, t), 0)
    memb = (row8 == gsel).astype(F32)
    before = _dot(memb.astype(BF16), tri_ref[...]) + run_ref[...]
    pos = jnp.sum(memb * before, axis=0, keepdims=True)
    run_ref[...] = run_ref[...] + jnp.sum(memb, axis=1, keepdims=True)
    cnt_ref[...] = run_ref[:, 0:LANES]
    idx_ref[0] = jnp.concatenate([gsel.astype(F32), pos, jnp.zeros((6, t), F32)], axis=0)

    bits = lax.bitcast_convert_type(h.astype(BF16).astype(F32), jnp.uint32)
    words = (bits[:, 0:d // 2] >> 16) | bits[:, d // 2:]
    h_sublanes = d // 2 // LANES
    for s in range(h_sublanes):
        tok_ref[_token_rows(s, t)] = words[:, s * LANES:(s + 1) * LANES]
    g16 = gate[N_GROUPS:N_GROUPS + N_EXPERTS]
    gate_rows = jnp.concatenate([g16, jnp.zeros((LANES - N_EXPERTS, t), F32)], axis=0).T
    tok_ref[_token_rows(h_sublanes, t)] = lax.bitcast_convert_type(gate_rows, jnp.uint32)
    for s in range(h_sublanes + 1, TOKEN_SUBLANES):
        tok_ref[_token_rows(s, t)] = jnp.zeros((t, LANES), jnp.uint32)


def _router(x, g, wrt, brt, tm):
    n, d = x.shape
    nt = n // tm
    assert d // 2 // LANES + 1 <= TOKEN_SUBLANES
    return pl.pallas_call(
        _router_kernel,
        out_shape=(jax.ShapeDtypeStruct((n * TOKEN_SUBLANES, LANES), jnp.uint32),
                   jax.ShapeDtypeStruct((nt, 8, tm), F32),
                   jax.ShapeDtypeStruct((8, LANES), F32)),
        grid=(nt,),
        in_specs=[pl.BlockSpec((tm, d), lambda i: (i, 0)),
                  pl.BlockSpec((1, d), lambda i: (0, 0)),
                  pl.BlockSpec((LANES, d), lambda i: (0, 0)),
                  pl.BlockSpec((LANES, 1), lambda i: (0, 0))],
        out_specs=(pl.BlockSpec((tm * TOKEN_SUBLANES, LANES), lambda i: (i, 0)),
                   pl.BlockSpec((1, 8, tm), lambda i: (i, 0, 0)),
                   pl.BlockSpec((8, LANES), lambda i: (0, 0))),
        scratch_shapes=[pltpu.VMEM((tm, tm), BF16), pltpu.VMEM((8, tm), F32)],
        compiler_params=_params("arbitrary"),
        name="moe_router",
    )(x, g.reshape(1, d), wrt, brt)


def _permute_kernel(idx_ref, src_ref, *rest, scatter):
    dst_ref, sem = rest[-2], rest[-1]
    rows = idx_ref.shape[2]
    base = pl.program_id(0) * rows

    def copy(r):
        j = idx_ref[0, 0, r]
        src_row, dst_row = (base + r, j) if scatter else (j, base + r)
        return pltpu.make_async_copy(src_ref.at[pl.ds(src_row, 1)], dst_ref.at[pl.ds(dst_row, 1)], sem)

    def start(r, carry):
        copy(r).start()
        return carry

    def wait(r, carry):
        copy(r).wait()
        return carry

    lax.fori_loop(0, rows, start, 0, unroll=PERMUTE_UNROLL)
    lax.fori_loop(0, rows, wait, 0, unroll=PERMUTE_UNROLL)


def _permute_tokens(idx, src, dst_tokens, scatter, init=None):
    n = idx.shape[0]
    rows = PERMUTE_ROWS
    tile = (TOKEN_SUBLANES, LANES)
    any_spec = pl.BlockSpec(memory_space=pl.ANY)
    ins = [idx.reshape(n // rows, 1, rows), src.reshape(-1, *tile)]
    if init is not None:
        ins.append(init.reshape(-1, *tile))
    out = pl.pallas_call(
        functools.partial(_permute_kernel, scatter=scatter),
        out_shape=jax.ShapeDtypeStruct((dst_tokens, *tile), src.dtype),
        grid=(n // rows,),
        in_specs=[pl.BlockSpec((1, 1, rows), lambda i: (i, 0, 0), memory_space=pltpu.SMEM), any_spec]
        + ([any_spec] if init is not None else []),
        out_specs=any_spec,
        scratch_shapes=[pltpu.SemaphoreType.DMA],
        input_output_aliases={2: 0} if init is not None else {},
        compiler_params=_params("arbitrary"),
        name="moe_scatter_tokens" if scatter else "moe_gather_tokens",
    )(*ins)
    return out.reshape(dst_tokens * TOKEN_SUBLANES, LANES)


def _experts_kernel(grp_ref, nblk_ref, tok_ref, w1_ref, w3_ref, w2_ref, o_ref):
    i = pl.program_id(0)
    tb = o_ref.shape[0] // TOKEN_SUBLANES
    d = w1_ref.shape[2]
    h_sublanes = d // 2 // LANES

    @pl.when(i < nblk_ref[0])
    def _():
        words = [tok_ref[_token_rows(s, tb)] for s in range(h_sublanes)]
        low = [lax.bitcast_convert_type(w << 16, F32).astype(BF16) for w in words]
        high = [lax.bitcast_convert_type(w & jnp.uint32(0xFFFF0000), F32).astype(BF16) for w in words]
        h = jnp.concatenate(low + high, axis=1)
        gates = lax.bitcast_convert_type(tok_ref[_token_rows(h_sublanes, tb)], F32)
        lane = lax.broadcasted_iota(jnp.int32, (1, LANES), 1)
        first = grp_ref[i] * EXPERTS_PER_GROUP
        acc = jnp.zeros((tb, d), F32)
        for e in range(EXPERTS_PER_GROUP):
            gate = jnp.sum(jnp.where(lane == first + e, gates, 0.0), axis=1, keepdims=True)
            h1 = _dot(h, w1_ref[0, e])
            hid = h1 * jax.nn.sigmoid(h1) * _dot(h, w3_ref[0, e]) * gate
            acc = acc + _dot(hid.astype(BF16), w2_ref[0, e])
        for s in range(TOKEN_SUBLANES):
            o_ref[_token_rows(s, tb)] = acc[:, s * LANES:(s + 1) * LANES]

    @pl.when(i >= nblk_ref[0])
    def _():
        o_ref[...] = jnp.zeros_like(o_ref)


def _experts(toks, blk_grp, nblk, w1, w3, w2):
    d, ff = w1.shape[-2:]
    assert d == TOKEN_SUBLANES * LANES
    tb = MOE_BLOCK
    epg = EXPERTS_PER_GROUP
    tile_rows = pl.BlockSpec((tb * TOKEN_SUBLANES, LANES), lambda i, grp, nb: (i, 0))
    grid_spec = pltpu.PrefetchScalarGridSpec(
        num_scalar_prefetch=2,
        grid=(toks.shape[0] // (tb * TOKEN_SUBLANES),),
        in_specs=[tile_rows,
                  pl.BlockSpec((1, epg, d, ff), lambda i, grp, nb: (grp[i], 0, 0, 0)),
                  pl.BlockSpec((1, epg, d, ff), lambda i, grp, nb: (grp[i], 0, 0, 0)),
                  pl.BlockSpec((1, epg, ff, d), lambda i, grp, nb: (grp[i], 0, 0, 0))],
        out_specs=tile_rows,
    )
    return pl.pallas_call(
        _experts_kernel,
        out_shape=jax.ShapeDtypeStruct(toks.shape, F32),
        grid_spec=grid_spec,
        compiler_params=_params("arbitrary"),
        name="moe_experts",
    )(blk_grp, nblk, toks, w1.reshape(N_GROUPS, epg, d, ff), w3.reshape(N_GROUPS, epg, d, ff),
      w2.reshape(N_GROUPS, epg, ff, d))


def _moe(x, g, w_grp, b_grp, w_exp, b_exp, w1, w3, w2):
    n, d = x.shape
    pad = LANES - N_GROUPS - N_EXPERTS
    wrt = jnp.concatenate([w_grp, w_exp, jnp.zeros((d, pad), F32)], axis=1).T
    brt = jnp.concatenate([b_grp, b_exp, jnp.zeros((pad,), F32)]).reshape(LANES, 1)
    toks, idx, cnt = _router(x, g, wrt, brt, MOE_ROWS)

    tb = MOE_BLOCK
    counts = cnt[0:N_GROUPS, 0].astype(jnp.int32)
    padded = (counts + tb - 1) // tb * tb
    ends = jnp.cumsum(padded)
    offs = ends - padded
    gsel = idx[:, 0, :].reshape(n).astype(jnp.int32)
    group_ids = jnp.arange(N_GROUPS, dtype=jnp.int32)
    dest = (jnp.sum(jnp.where(gsel[:, None] == group_ids[None, :], offs[None, :], 0), axis=1)
            + idx[:, 1, :].reshape(n).astype(jnp.int32))
    nblk_max = n // tb + N_GROUPS
    blk_start = jnp.arange(nblk_max, dtype=jnp.int32) * tb
    blk_grp = jnp.minimum(jnp.sum(blk_start[:, None] >= ends[None, :], axis=1), N_GROUPS - 1).astype(jnp.int32)
    nblk = (ends[N_GROUPS - 1:] // tb).astype(jnp.int32)

    sorted_toks = _permute_tokens(dest, toks, nblk_max * tb, True,
                                  init=jnp.zeros((nblk_max * tb * TOKEN_SUBLANES, LANES), jnp.uint32))
    ys = _experts(sorted_toks, blk_grp, nblk, w1, w3, w2)
    return _permute_tokens(dest, ys, n, False)


def _final_norm_kernel(x_ref, y_ref, g_ref, o_ref):
    o_ref[...] = _rms(x_ref[...] + _from_token_tiles(y_ref), g_ref[...])


def _final_norm(x, y, g, tm):
    n, d = x.shape
    row = pl.BlockSpec((tm, d), lambda i: (i, 0))
    return pl.pallas_call(
        _final_norm_kernel,
        out_shape=jax.ShapeDtypeStruct((n, d), F32),
        grid=(n // tm,),
        in_specs=[row, pl.BlockSpec((tm * TOKEN_SUBLANES, LANES), lambda i: (i, 0)),
                  pl.BlockSpec((1, d), lambda i: (0, 0))],
        out_specs=row,
        compiler_params=_params("parallel"),
        name="final_norm",
    )(x, y, g.reshape(1, d))


def kernel(x, mem, rel_bias, mem_norm_g, final_norm_g, hgrn_lb, norm_mix_g, norm_xattn_g, norm_ffn_g, even_w_in, even_w_out, hgrn_norm_g, conv_dw, conv_db, conv_ln_g, conv_ln_b, odd_w_qkv, odd_w_out, lam_q1, lam_k1, lam_q2, lam_k2, subln_g, xa_wq, xa_wkv, xa_wo, moe_w_grp, moe_b_grp, moe_w_exp, moe_b_exp, moe_w1, moe_w3, moe_w2):
    batch, seq, d = x.shape
    mem_len = mem.shape[1]
    depth = norm_mix_g.shape[0]
    n = batch * seq
    assert seq % ATT_TILE == 0 and seq % HGRN_ROWS == 0 and seq % CONV_ROWS == 0 and seq % ROW_TILE == 0
    assert n % MOE_ROWS == 0 and n % MOE_BLOCK == 0 and n % PERMUTE_ROWS == 0
    assert (batch * mem_len) % ROW_TILE == 0

    xf = x.reshape(n, d)
    memf = mem.reshape(batch * mem_len, d)
    bias_tiles = _bias_tiles(rel_bias)
    y_moe = None

    for i in range(depth):
        j = i // 2
        w_in = (even_w_in if i % 2 == 0 else odd_w_qkv)[j].astype(BF16)
        if y_moe is None:
            proj = _norm_matmul(xf, norm_mix_g[i], w_in, ROW_TILE)
        else:
            xf, proj = _add_norm_matmul(xf, y_moe, norm_mix_g[i], w_in, ROW_TILE)
        if i % 2 == 0:
            o_hgrn = _hgrn(proj, hgrn_lb, hgrn_norm_g[j], j, batch, seq)
            o_conv = _conv(proj, conv_dw[j], conv_db[j], conv_ln_g[j], conv_ln_b[j], batch, seq)
            xf = _resid_matmul(xf, [o_hgrn, o_conv], even_w_out[j].astype(BF16), ROW_TILE)
        else:
            lam_init = 0.8 - 0.6 * math.exp(-0.3 * i)
            lamv = jnp.stack([lam_q1[j], lam_k1[j], lam_q2[j], lam_k2[j]]).astype(F32)
            o_att = _diff_attn(proj, bias_tiles, lamv, subln_g[j], lam_init, batch, seq)
            xf = _resid_matmul(xf, [o_att], odd_w_out[j].astype(BF16), ROW_TILE)

        kv = _norm_matmul(memf, mem_norm_g, xa_wkv[i].astype(BF16), ROW_TILE)
        xf = _xattn(xf, norm_xattn_g[i], xa_wq[i].astype(BF16), kv, xa_wo[i].astype(BF16), seq, mem_len, ROW_TILE)

        y_moe = _moe(xf, norm_ffn_g[i], moe_w_grp[i], moe_b_grp[i], moe_w_exp[i], moe_b_exp[i],
                     moe_w1[i].astype(BF16), moe_w3[i].astype(BF16), moe_w2[i].astype(BF16))

    return _final_norm(xf, y_moe, final_norm_g, ROW_TILE).reshape(batch, seq, d)
```

```python
import functools
import math

import jax
import jax.numpy as jnp
from jax import lax
from jax.experimental import pallas as pl
from jax.experimental.pallas import tpu as pltpu

F32 = jnp.float32
BF16 = jnp.bfloat16

EPS = 1e-6
NEG_BIG = -1e30
TINY = 1e-30

HGRN_HEADS = 4
HGRN_DK = 128
HGRN_WIDTH = HGRN_HEADS * HGRN_DK
CONV_WIDTH = 31
DA_HEADS = 8
DA_HEAD_DIM = 64
DA_VDIM = 2 * DA_HEAD_DIM
REL_BUCKETS = 32
REL_MAX_DIST = 128
XA_HEADS = 4
N_GROUPS = 4
EXPERTS_PER_GROUP = 4
N_EXPERTS = N_GROUPS * EXPERTS_PER_GROUP

LANES = 128
TOKEN_SUBLANES = 8
V7X_VMEM_LIMIT_BYTES = 56 * 1024 * 1024

ROW_TILE = 512
HGRN_CHUNK = 128
HGRN_SUB = 16
HGRN_ROWS = 512
CONV_ROWS = 512
CONV_HALO = 32
ATT_TILE = 512
ATT_STRIP = 256
MOE_ROWS = 1024
ROUTE_ROWS = 24
MOE_BLOCK = 1024
PERMUTE_ROWS = 1024
PERMUTE_UNROLL = 8


def _params(*sem):
    return pltpu.CompilerParams(dimension_semantics=sem, vmem_limit_bytes=V7X_VMEM_LIMIT_BYTES)


def _rms(x, g):
    return x * lax.rsqrt(jnp.mean(x * x, axis=-1, keepdims=True) + EPS) * g


def _dot(a, b):
    return jnp.dot(a, b, preferred_element_type=F32)


def _dot_nt(a, b):
    return lax.dot_general(a, b, (((1,), (1,)), ((), ())), preferred_element_type=F32)


def _token_rows(s, n):
    return (pl.ds(s, n, stride=TOKEN_SUBLANES), slice(None))


def _from_token_tiles(y_ref):
    n = y_ref.shape[0] // TOKEN_SUBLANES
    return jnp.concatenate([y_ref[_token_rows(s, n)] for s in range(TOKEN_SUBLANES)], axis=1)


def _norm_matmul_kernel(x_ref, g_ref, w_ref, o_ref):
    h = _rms(x_ref[...], g_ref[...]).astype(BF16)
    o_ref[...] = _dot(h, w_ref[...]).astype(o_ref.dtype)


def _norm_matmul(x, g, w, tm):
    n, d = x.shape
    nout = w.shape[1]
    return pl.pallas_call(
        _norm_matmul_kernel,
        out_shape=jax.ShapeDtypeStruct((n, nout), BF16),
        grid=(n // tm,),
        in_specs=[pl.BlockSpec((tm, d), lambda i: (i, 0)),
                  pl.BlockSpec((1, d), lambda i: (0, 0)),
                  pl.BlockSpec((d, nout), lambda i: (0, 0))],
        out_specs=pl.BlockSpec((tm, nout), lambda i: (i, 0)),
        compiler_params=_params("parallel"),
        name="norm_matmul",
    )(x, g.reshape(1, d), w)


def _add_norm_matmul_kernel(x_ref, y_ref, g_ref, w_ref, xo_ref, o_ref):
    x = x_ref[...] + _from_token_tiles(y_ref)
    xo_ref[...] = x
    o_ref[...] = _dot(_rms(x, g_ref[...]).astype(BF16), w_ref[...]).astype(o_ref.dtype)


def _add_norm_matmul(x, y, g, w, tm):
    n, d = x.shape
    nout = w.shape[1]
    row = pl.BlockSpec((tm, d), lambda i: (i, 0))
    tiles = pl.BlockSpec((tm * TOKEN_SUBLANES, LANES), lambda i: (i, 0))
    return pl.pallas_call(
        _add_norm_matmul_kernel,
        out_shape=(jax.ShapeDtypeStruct((n, d), F32), jax.ShapeDtypeStruct((n, nout), BF16)),
        grid=(n // tm,),
        in_specs=[row, tiles, pl.BlockSpec((1, d), lambda i: (0, 0)), pl.BlockSpec((d, nout), lambda i: (0, 0))],
        out_specs=(row, pl.BlockSpec((tm, nout), lambda i: (i, 0))),
        compiler_params=_params("parallel"),
        name="add_norm_matmul",
    )(x, y, g.reshape(1, d), w)


def _resid_matmul_kernel(*refs):
    x_ref, a_refs, w_ref, o_ref = refs[0], refs[1:-2], refs[-2], refs[-1]
    a = [r[...] for r in a_refs]
    a = a[0] if len(a) == 1 else jnp.concatenate(a, axis=1)
    o_ref[...] = x_ref[...] + _dot(a, w_ref[...])


def _resid_matmul(x, acts, w, tm):
    n, d = x.shape
    return pl.pallas_call(
        _resid_matmul_kernel,
        out_shape=jax.ShapeDtypeStruct((n, d), F32),
        grid=(n // tm,),
        in_specs=[pl.BlockSpec((tm, d), lambda i: (i, 0))]
        + [pl.BlockSpec((tm, a.shape[1]), lambda i: (i, 0)) for a in acts]
        + [pl.BlockSpec(w.shape, lambda i: (0, 0))],
        out_specs=pl.BlockSpec((tm, d), lambda i: (i, 0)),
        compiler_params=_params("parallel"),
        name="resid_matmul",
    )(x, *acts, w)


def _cumsum_rows(g):
    rows = lax.broadcasted_iota(jnp.int32, g.shape, 0)
    b, sh = g, 1
    while sh < g.shape[0]:
        b = b + jnp.where(rows >= sh, pltpu.roll(b, sh, 0), 0.0)
        sh *= 2
    return b


def _hgrn_chunk(zq, zf, v, lb, st):
    c = zq.shape[0]
    f = lb + (1.0 - lb) * jax.nn.sigmoid(zf)
    g = jnp.log(jnp.maximum(f, TINY))
    kk = 1.0 - f
    q = zq * jax.nn.sigmoid(zq)
    b = _cumsum_rows(g)

    o = _dot_nt((q * jnp.exp(b)).astype(BF16), st.astype(BF16))
    b_last = b[c - 1:c]
    kdec = (kk * jnp.exp(b_last - b)).astype(BF16)
    st_new = st * jnp.exp(b_last) + _dot(v.T.astype(BF16), kdec)

    rows = lax.broadcasted_iota(jnp.int32, (HGRN_SUB, 1), 0)
    half = HGRN_SUB // 2
    a_rows, d_rows = [], []
    for i in range(c // HGRN_SUB):
        r0 = i * HGRN_SUB
        bi, qi, ki, vi = (t[r0:r0 + HGRN_SUB] for t in (b, q, kk, v))
        if i == 0:
            a_rows.append(jnp.zeros((HGRN_SUB, c), F32))
        else:
            beta = b[r0 - 1:r0]
            qs = (qi * jnp.exp(bi - beta)).astype(BF16)
            ks = (kk[:r0] * jnp.exp(beta - b[:r0])).astype(BF16)
            ks = jnp.concatenate([ks, jnp.zeros((c - r0, ks.shape[1]), BF16)], axis=0)
            a_rows.append(_dot_nt(qs, ks))
        top = jnp.zeros((half, v.shape[1]), F32)
        bot = jnp.zeros((half, v.shape[1]), F32)
        for s in range(HGRN_SUB):
            lo = 0 if s < half else half
            w = qi[lo:] * jnp.exp(jnp.minimum(bi[lo:] - bi[s:s + 1], 0.0)) * ki[s:s + 1]
            rs = jnp.where(rows[lo:] >= s, jnp.sum(w, axis=1, keepdims=True), 0.0)
            contrib = rs * vi[s:s + 1]
            if lo == 0:
                top = top + contrib[:half]
                bot = bot + contrib[half:]
            else:
                bot = bot + contrib
        d_rows += [top, bot]
    a = jnp.concatenate(a_rows, axis=0).astype(BF16)
    o = o + _dot(a, v.astype(BF16)) + jnp.concatenate(d_rows, axis=0)
    return o, st_new


def _hgrn_kernel(zq_ref, zf_ref, zi_ref, zg_ref, lbp_ref, gn_ref, o_ref, st_ref, *, layer):
    @pl.when(pl.program_id(2) == 0)
    def _():
        st_ref[...] = jnp.zeros_like(st_ref)

    lbp = lbp_ref[...]
    e = jnp.exp(lbp - jnp.max(lbp, axis=0, keepdims=True))
    p = e / jnp.sum(e, axis=0, keepdims=True)
    lb = jnp.zeros((1, lbp.shape[1]), F32)
    for r in range(1, layer + 1):
        lb = lb + p[r:r + 1]
    gn = gn_ref[...]

    def chunk(ci, carry):
        r0 = pl.multiple_of(ci * HGRN_CHUNK, HGRN_CHUNK)
        sl = pl.ds(r0, HGRN_CHUNK)
        o, st_new = _hgrn_chunk(zq_ref[sl, :].astype(F32), zf_ref[sl, :].astype(F32),
                                zi_ref[sl, :].astype(F32), lb, st_ref[...])
        st_ref[...] = st_new
        o = _rms(o, gn) * jax.nn.sigmoid(zg_ref[sl, :].astype(F32))
        o_ref[sl, :] = o.astype(o_ref.dtype)
        return carry

    lax.fori_loop(0, o_ref.shape[0] // HGRN_CHUNK, chunk, 0)


def _hgrn(proj, hgrn_lb, gn, layer, batch, seq):
    n = proj.shape[0]
    ns = seq // HGRN_ROWS
    hh = HGRN_HEADS

    def col(k):
        return pl.BlockSpec((HGRN_ROWS, HGRN_DK), lambda b, h, s: (b * ns + s, k * hh + h))

    return pl.pallas_call(
        functools.partial(_hgrn_kernel, layer=layer),
        out_shape=jax.ShapeDtypeStruct((n, HGRN_WIDTH), BF16),
        grid=(batch, hh, ns),
        in_specs=[col(0), col(1), col(2), col(3),
                  pl.BlockSpec((hgrn_lb.shape[0], HGRN_DK), lambda b, h, s: (0, h)),
                  pl.BlockSpec((1, HGRN_DK), lambda b, h, s: (0, 0))],
        out_specs=pl.BlockSpec((HGRN_ROWS, HGRN_DK), lambda b, h, s: (b * ns + s, h)),
        scratch_shapes=[pltpu.VMEM((HGRN_DK, HGRN_DK), F32)],
        compiler_params=_params("parallel", "parallel", "arbitrary"),
        name="hgrn2",
    )(proj, proj, proj, proj, hgrn_lb, gn.reshape(1, HGRN_DK))


def _conv_kernel(cu_ref, cv_ref, dw_ref, db_ref, g_ref, b_ref, o_ref, ext_ref):
    rows = o_ref.shape[0]

    @pl.when(pl.program_id(1) == 0)
    def _():
        ext_ref[0:CONV_HALO, :] = jnp.zeros((CONV_HALO, ext_ref.shape[1]), F32)

    @pl.when(pl.program_id(1) > 0)
    def _():
        ext_ref[0:CONV_HALO, :] = ext_ref[rows:rows + CONV_HALO, :]

    ext_ref[CONV_HALO:, :] = cu_ref[...].astype(F32) * jax.nn.sigmoid(cv_ref[...].astype(F32))
    acc = jnp.zeros(o_ref.shape, F32) + db_ref[...]
    first = CONV_HALO - (CONV_WIDTH - 1)
    for j in range(CONV_WIDTH):
        acc = acc + dw_ref[j:j + 1, :] * ext_ref[first + j:first + j + rows, :]
    mu = jnp.mean(acc, axis=-1, keepdims=True)
    cen = acc - mu
    var = jnp.mean(cen * cen, axis=-1, keepdims=True)
    y = cen * lax.rsqrt(var + EPS) * g_ref[...] + b_ref[...]
    o_ref[...] = (y * jax.nn.sigmoid(y)).astype(o_ref.dtype)


def _conv(proj, dw, db, ln_g, ln_b, batch, seq):
    n = proj.shape[0]
    ch = dw.shape[1]
    ns = seq // CONV_ROWS
    first_blk = 4 * HGRN_WIDTH // ch
    vec = pl.BlockSpec((1, ch), lambda b, s: (0, 0))
    return pl.pallas_call(
        _conv_kernel,
        out_shape=jax.ShapeDtypeStruct((n, ch), BF16),
        grid=(batch, ns),
        in_specs=[pl.BlockSpec((CONV_ROWS, ch), lambda b, s: (b * ns + s, first_blk)),
                  pl.BlockSpec((CONV_ROWS, ch), lambda b, s: (b * ns + s, first_blk + 1)),
                  pl.BlockSpec((CONV_WIDTH, ch), lambda b, s: (0, 0)), vec, vec, vec],
        out_specs=pl.BlockSpec((CONV_ROWS, ch), lambda b, s: (b * ns + s, 0)),
        scratch_shapes=[pltpu.VMEM((CONV_ROWS + CONV_HALO, ch), F32)],
        compiler_params=_params("parallel", "arbitrary"),
        name="glu_conv_ln",
    )(proj, proj, dw, db.reshape(1, ch), ln_g.reshape(1, ch), ln_b.reshape(1, ch))


def _bias_kernel(tab_ref, o_ref):
    h = pl.program_id(0)
    t = o_ref.shape[2]
    k = lax.broadcasted_iota(jnp.int32, (t, t), 0)
    q = lax.broadcasted_iota(jnp.int32, (t, t), 1)
    max_exact = REL_BUCKETS // 2
    for d in range(2):
        rel = q - k + d * t
        nn = jnp.maximum(rel, 0)
        nf = jnp.maximum(nn, 1).astype(F32)
        large = max_exact + (jnp.log(nf / max_exact) / math.log(REL_MAX_DIST / max_exact)
                             * (REL_BUCKETS - max_exact)).astype(jnp.int32)
        large = jnp.clip(large, 0, REL_BUCKETS - 1)
        bucket = jnp.where(nn < max_exact, nn, large)
        val = jnp.zeros((t, t), F32)
        for bk in range(REL_BUCKETS):
            val = jnp.where(bucket == bk, tab_ref[bk, h], val)
        if d == 0:
            val = jnp.where(rel >= 0, val, NEG_BIG)
        o_ref[0, d] = val
    o_ref[0, 2] = jnp.zeros((t, t), F32) + tab_ref[REL_BUCKETS - 1, h]


def _bias_tiles(rel_bias):
    assert ATT_TILE >= REL_MAX_DIST
    heads = rel_bias.shape[1]
    return pl.pallas_call(
        _bias_kernel,
        out_shape=jax.ShapeDtypeStruct((heads, 3, ATT_TILE, ATT_TILE), F32),
        grid=(heads,),
        in_specs=[pl.BlockSpec(memory_space=pltpu.SMEM)],
        out_specs=pl.BlockSpec((1, 3, ATT_TILE, ATT_TILE), lambda h: (h, 0, 0, 0)),
        compiler_params=_params("parallel"),
        name="t5_bias_tiles",
    )(rel_bias)


def _attn_kernel(q_ref, k_ref, v_ref, bias_ref, lamv_ref, g_ref, o_ref,
                 vt_ref, qs_ref, m_ref, l_ref, acc_ref, sa_ref, sb_ref, cma_ref, cmb_ref, *, lam_init):
    i = pl.program_id(2)
    t = o_ref.shape[0]

    @pl.when(i == 0)
    def _():
        for j in range(vt_ref.shape[0]):
            vt_ref[j] = v_ref[j * t:(j + 1) * t, :].astype(F32).T.astype(BF16)

    lane = lax.broadcasted_iota(jnp.int32, (1, DA_VDIM), 1)
    q = q_ref[...].astype(F32) * (DA_HEAD_DIM ** -0.5)
    qs_ref[0] = jnp.where(lane < DA_HEAD_DIM, q, 0.0).astype(BF16)
    qs_ref[1] = jnp.where(lane >= DA_HEAD_DIM, q, 0.0).astype(BF16)
    m_ref[...] = jnp.full(m_ref.shape, -jnp.inf, F32)
    l_ref[...] = jnp.zeros(l_ref.shape, F32)
    acc_ref[...] = jnp.zeros(acc_ref.shape, F32)

    def produce(j, s_dst, cm_dst):
        kb = k_ref[pl.ds(pl.multiple_of(j * t, t), t), :]
        bt = bias_ref[0, jnp.minimum(i - j, 2)]
        for mp in range(2):
            s = _dot_nt(kb, qs_ref[mp]) + bt
            s_dst[mp] = s
            cm_dst[mp] = jnp.max(s, axis=0, keepdims=True)

    def consume(j, s_src, cm_src):
        vt = vt_ref[j]
        for mp in range(2):
            for st in range(t // ATT_STRIP):
                cols = slice(st * ATT_STRIP, (st + 1) * ATT_STRIP)
                m_old = m_ref[mp, :, cols]
                mn = jnp.maximum(m_old, cm_src[mp, :, cols])
                p = jnp.exp(s_src[mp, :, cols] - mn)
                al = jnp.exp(m_old - mn)
                m_ref[mp, :, cols] = mn
                l_ref[mp, :, cols] = l_ref[mp, :, cols] * al + jnp.sum(p, axis=0, keepdims=True)
                acc_ref[mp, :, cols] = acc_ref[mp, :, cols] * al + _dot(vt, p.astype(BF16))

    produce(0, sa_ref, cma_ref)

    def pair(pi, carry):
        j = 2 * pi
        produce(j + 1, sb_ref, cmb_ref)
        consume(j, sa_ref, cma_ref)
        produce(j + 2, sa_ref, cma_ref)
        consume(j + 1, sb_ref, cmb_ref)
        return carry

    lax.fori_loop(0, i >> 1, pair, 0)

    @pl.when((i & 1) == 0)
    def _():
        consume(i, sa_ref, cma_ref)

    @pl.when((i & 1) == 1)
    def _():
        produce(i, sb_ref, cmb_ref)
        consume(i - 1, sa_ref, cma_ref)
        consume(i, sb_ref, cmb_ref)

    lv = lamv_ref[...]
    lam = (jnp.exp(jnp.sum(lv[0:1] * lv[1:2], axis=1, keepdims=True))
           - jnp.exp(jnp.sum(lv[2:3] * lv[3:4], axis=1, keepdims=True)) + lam_init)
    o = acc_ref[0] * (1.0 / l_ref[0]) - lam * (acc_ref[1] * (1.0 / l_ref[1]))
    y = o * lax.rsqrt(jnp.mean(o * o, axis=0, keepdims=True) + EPS) * g_ref[...] * (1.0 - lam_init)
    o_ref[...] = y.T.astype(o_ref.dtype)


def _diff_attn(qkv, bias_tiles, lamv, subln_g, lam_init, batch, seq):
    n = qkv.shape[0]
    t = ATT_TILE
    nq = seq // t
    hh = DA_HEADS
    return pl.pallas_call(
        functools.partial(_attn_kernel, lam_init=lam_init),
        out_shape=jax.ShapeDtypeStruct((n, hh * DA_VDIM), BF16),
        grid=(batch, hh, nq),
        in_specs=[pl.BlockSpec((t, DA_VDIM), lambda b, h, i: (b * nq + i, h)),
                  pl.BlockSpec((seq, DA_VDIM), lambda b, h, i: (b, hh + h)),
                  pl.BlockSpec((seq, DA_VDIM), lambda b, h, i: (b, 2 * hh + h)),
                  pl.BlockSpec((1, 3, t, t), lambda b, h, i: (h, 0, 0, 0)),
                  pl.BlockSpec(lamv.shape, lambda b, h, i: (0, 0)),
                  pl.BlockSpec((DA_VDIM, 1), lambda b, h, i: (0, 0))],
        out_specs=pl.BlockSpec((t, DA_VDIM), lambda b, h, i: (b * nq + i, h)),
        scratch_shapes=[pltpu.VMEM((nq, DA_VDIM, t), BF16), pltpu.VMEM((2, t, DA_VDIM), BF16),
                        pltpu.VMEM((2, 1, t), F32), pltpu.VMEM((2, 1, t), F32),
                        pltpu.VMEM((2, DA_VDIM, t), F32),
                        pltpu.VMEM((2, t, t), F32), pltpu.VMEM((2, t, t), F32),
                        pltpu.VMEM((2, 1, t), F32), pltpu.VMEM((2, 1, t), F32)],
        compiler_params=_params("parallel", "parallel", "arbitrary"),
        name="diff_attention",
    )(qkv, qkv, qkv, bias_tiles, lamv, subln_g.reshape(DA_VDIM, 1))


def _xattn_kernel(x_ref, g_ref, wq_ref, kv_ref, wo_ref, o_ref):
    x = x_ref[...]
    d = x.shape[1]
    dh = d // XA_HEADS
    h = _rms(x, g_ref[...]).astype(BF16)
    q = (_dot(h, wq_ref[...]) * (dh ** -0.5)).astype(BF16)
    kv = kv_ref[...]
    outs = []
    for hd in range(XA_HEADS):
        s = _dot_nt(q[:, hd * dh:(hd + 1) * dh], kv[:, hd * dh:(hd + 1) * dh])
        p = jnp.exp(s - jnp.max(s, axis=-1, keepdims=True))
        pv = _dot(p.astype(BF16), kv[:, d + hd * dh:d + (hd + 1) * dh])
        outs.append(pv * (1.0 / jnp.sum(p, axis=-1, keepdims=True)))
    o = jnp.concatenate(outs, axis=1).astype(BF16)
    o_ref[...] = x + _dot(o, wo_ref[...])


def _xattn(x, g, wq, kv, wo, seq, mem_len, tm):
    n, d = x.shape
    per_batch = seq // tm
    return pl.pallas_call(
        _xattn_kernel,
        out_shape=jax.ShapeDtypeStruct((n, d), F32),
        grid=(n // tm,),
        in_specs=[pl.BlockSpec((tm, d), lambda i: (i, 0)),
                  pl.BlockSpec((1, d), lambda i: (0, 0)),
                  pl.BlockSpec((d, d), lambda i: (0, 0)),
                  pl.BlockSpec((mem_len, 2 * d), lambda i: (i // per_batch, 0)),
                  pl.BlockSpec((d, d), lambda i: (0, 0))],
        out_specs=pl.BlockSpec((tm, d), lambda i: (i, 0)),
        compiler_params=_params("parallel"),
        name="memory_xattn",
    )(x, g.reshape(1, d), wq, kv, wo)


def _route_t(lg):
    row = lax.broadcasted_iota(jnp.int32, lg.shape, 0)
    big = lg.shape[0]
    is_grp = row < N_GROUPS
    gl = jnp.where(is_grp, lg, -jnp.inf)
    gmax = jnp.max(gl, axis=0, keepdims=True)
    gsel = jnp.min(jnp.where(gl == gmax, row, big), axis=0, keepdims=True)
    g_w = 1.0 / jnp.sum(jnp.where(is_grp, jnp.exp(lg - gmax), 0.0), axis=0, keepdims=True)
    e_idx = row - N_GROUPS
    in_grp = (e_idx >= 0) & (e_idx < N_EXPERTS) & (e_idx // EXPERTS_PER_GROUP == gsel)
    ml = jnp.where(in_grp, lg, -jnp.inf)
    v1 = jnp.max(ml, axis=0, keepdims=True)
    i1 = jnp.min(jnp.where(ml == v1, row, big), axis=0, keepdims=True)
    ml2 = jnp.where(row == i1, -jnp.inf, ml)
    v2 = jnp.max(ml2, axis=0, keepdims=True)
    i2 = jnp.min(jnp.where(ml2 == v2, row, big), axis=0, keepdims=True)
    t = jnp.exp(v2 - v1)
    w1 = g_w / (1.0 + t)
    w2 = g_w * t / (1.0 + t)
    return gsel, jnp.where(row == i1, w1, 0.0) + jnp.where(row == i2, w2, 0.0)


def _router_kernel(x_ref, g_ref, wrt_ref, brt_ref, tok_ref, idx_ref, cnt_ref, tri_ref, run_ref):
    t, d = x_ref.shape

    @pl.when(pl.program_id(0) == 0)
    def _():
        r = lax.broadcasted_iota(jnp.int32, (t, t), 0)
        c = lax.broadcasted_iota(jnp.int32, (t, t), 1)
        tri_ref[...] = (r < c).astype(BF16)
        run_ref[...] = jnp.zeros_like(run_ref)

    h = _rms(x_ref[...], g_ref[...])
    logits = lax.dot_general(wrt_ref[...], h, (((1,), (1,)), ((), ())), preferred_element_type=F32,
                             precision=lax.Precision.HIGHEST) + brt_ref[...]
    gsel, gate = _route_t(logits[0:ROUTE_ROWS])

    row8 = lax.broadcasted_iota(jnp.int32, (8, t), 0)
    memb = (row8 == gsel).astype(F32)
    before = _dot(memb.astype(BF16), tri_ref[...]) + run_ref[...]
    pos = jnp.sum(memb * before, axis=0, keepdims=True)
    run_ref[...] = run_ref[...] + jnp.sum(memb, axis=1, keepdims=True)
    cnt_ref[...] = run_ref[:, 0:LANES]
    idx_ref[0] = jnp.concatenate([gsel.astype(F32), pos, jnp.zeros((6, t), F32)], axis=0)

    bits = lax.bitcast_convert_type(h.astype(BF16).astype(F32), jnp.uint32)
    words = (bits[:, 0:d // 2] >> 16) | bits[:, d // 2:]
    h_sublanes = d // 2 // LANES
    for s in range(h_sublanes):
        tok_ref[_token_rows(s, t)] = words[:, s * LANES:(s + 1) * LANES]
    g16 = gate[N_GROUPS:N_GROUPS + N_EXPERTS]
    gate_rows = jnp.concatenate([g16, jnp.zeros((LANES - N_EXPERTS, t), F32)], axis=0).T
    tok_ref[_token_rows(h_sublanes, t)] = lax.bitcast_convert_type(gate_rows, jnp.uint32)
    for s in range(h_sublanes + 1, TOKEN_SUBLANES):
        tok_ref[_token_rows(s, t)] = jnp.zeros((t, LANES), jnp.uint32)


def _router(x, g, wrt, brt, tm):
    n, d = x.shape
    nt = n // tm
    assert d // 2 // LANES + 1 <= TOKEN_SUBLANES
    return pl.pallas_call(
        _router_kernel,
        out_shape=(jax.ShapeDtypeStruct((n * TOKEN_SUBLANES, LANES), jnp.uint32),
                   jax.ShapeDtypeStruct((nt, 8, tm), F32),
                   jax.ShapeDtypeStruct((8, LANES), F32)),
        grid=(nt,),
        in_specs=[pl.BlockSpec((tm, d), lambda i: (i, 0)),
                  pl.BlockSpec((1, d), lambda i: (0, 0)),
                  pl.BlockSpec((LANES, d), lambda i: (0, 0)),
                  pl.BlockSpec((LANES, 1), lambda i: (0, 0))],
        out_specs=(pl.BlockSpec((tm * TOKEN_SUBLANES, LANES), lambda i: (i, 0)),
                   pl.BlockSpec((1, 8, tm), lambda i: (i, 0, 0)),
                   pl.BlockSpec((8, LANES), lambda i: (0, 0))),
        scratch_shapes=[pltpu.VMEM((tm, tm), BF16), pltpu.VMEM((8, tm), F32)],
        compiler_params=_params("arbitrary"),
        name="moe_router",
    )(x, g.reshape(1, d), wrt, brt)


def _permute_kernel(idx_ref, src_ref, *rest, scatter):
    dst_ref, sem = rest[-2], rest[-1]
    rows = idx_ref.shape[2]

    def copy(r):
        j = idx_ref[0, 0, r]
        src_row, dst_row = (r, j) if scatter else (j, r)
        return pltpu.make_async_copy(src_ref.at[pl.ds(src_row, 1)], dst_ref.at[pl.ds(dst_row, 1)], sem)

    def start(k, carry):
        for u in range(PERMUTE_UNROLL):
            copy(k * PERMUTE_UNROLL + u).start(priority=u % 2)
        return carry

    def wait(r, carry):
        copy(r).wait()
        return carry

    lax.fori_loop(0, rows // PERMUTE_UNROLL, start, 0)
    lax.fori_loop(0, rows, wait, 0, unroll=PERMUTE_UNROLL)


def _permute_tokens(idx, src, dst_tokens, scatter, init=None):
    n = idx.shape[0]
    rows = PERMUTE_ROWS
    tile = (TOKEN_SUBLANES, LANES)
    any_spec = pl.BlockSpec(memory_space=pl.ANY)
    block = pl.BlockSpec((rows, *tile), lambda i: (i, 0, 0))
    ins = [idx.reshape(n // rows, 1, rows), src.reshape(-1, *tile)]
    if init is not None:
        ins.append(init.reshape(-1, *tile))
    out = pl.pallas_call(
        functools.partial(_permute_kernel, scatter=scatter),
        out_shape=jax.ShapeDtypeStruct((dst_tokens, *tile), src.dtype),
        grid=(n // rows,),
        in_specs=[pl.BlockSpec((1, 1, rows), lambda i: (i, 0, 0), memory_space=pltpu.SMEM),
                  block if scatter else any_spec] + ([any_spec] if init is not None else []),
        out_specs=any_spec if scatter else block,
        scratch_shapes=[pltpu.SemaphoreType.DMA],
        input_output_aliases={2: 0} if init is not None else {},
        compiler_params=_params("arbitrary"),
        name="moe_scatter_tokens" if scatter else "moe_gather_tokens",
    )(*ins)
    return out.reshape(dst_tokens * TOKEN_SUBLANES, LANES)


def _experts_kernel(grp_ref, nblk_ref, tok_ref, w1_ref, w3_ref, w2_ref, o_ref):
    i = pl.program_id(0)
    tb = o_ref.shape[0] // TOKEN_SUBLANES
    d = w1_ref.shape[2]
    h_sublanes = d // 2 // LANES

    @pl.when(i < nblk_ref[0])
    def _():
        words = [tok_ref[_token_rows(s, tb)] for s in range(h_sublanes)]
        low = [lax.bitcast_convert_type(w << 16, F32).astype(BF16) for w in words]
        high = [lax.bitcast_convert_type(w & jnp.uint32(0xFFFF0000), F32).astype(BF16) for w in words]
        h = jnp.concatenate(low + high, axis=1)
        gates = lax.bitcast_convert_type(tok_ref[_token_rows(h_sublanes, tb)], F32)
        lane = lax.broadcasted_iota(jnp.int32, (1, LANES), 1)
        first = grp_ref[i] * EXPERTS_PER_GROUP
        acc = jnp.zeros((tb, d), F32)
        for e in range(EXPERTS_PER_GROUP):
            gate = jnp.sum(jnp.where(lane == first + e, gates, 0.0), axis=1, keepdims=True)
            h1 = _dot(h, w1_ref[0, e])
            hid = h1 * jax.nn.sigmoid(h1) * _dot(h, w3_ref[0, e]) * gate
            acc = acc + _dot(hid.astype(BF16), w2_ref[0, e])
        for s in range(TOKEN_SUBLANES):
            o_ref[_token_rows(s, tb)] = acc[:, s * LANES:(s + 1) * LANES]

    @pl.when(i >= nblk_ref[0])
    def _():
        o_ref[...] = jnp.zeros_like(o_ref)


def _experts(toks, blk_grp, nblk, w1, w3, w2):
    d, ff = w1.shape[-2:]
    assert d == TOKEN_SUBLANES * LANES
    tb = MOE_BLOCK
    epg = EXPERTS_PER_GROUP
    tile_rows = pl.BlockSpec((tb * TOKEN_SUBLANES, LANES), lambda i, grp, nb: (i, 0))
    grid_spec = pltpu.PrefetchScalarGridSpec(
        num_scalar_prefetch=2,
        grid=(toks.shape[0] // (tb * TOKEN_SUBLANES),),
        in_specs=[tile_rows,
                  pl.BlockSpec((1, epg, d, ff), lambda i, grp, nb: (grp[i], 0, 0, 0)),
                  pl.BlockSpec((1, epg, d, ff), lambda i, grp, nb: (grp[i], 0, 0, 0)),
                  pl.BlockSpec((1, epg, ff, d), lambda i, grp, nb: (grp[i], 0, 0, 0))],
        out_specs=tile_rows,
    )
    return pl.pallas_call(
        _experts_kernel,
        out_shape=jax.ShapeDtypeStruct(toks.shape, F32),
        grid_spec=grid_spec,
        compiler_params=_params("arbitrary"),
        name="moe_experts",
    )(blk_grp, nblk, toks, w1.reshape(N_GROUPS, epg, d, ff), w3.reshape(N_GROUPS, epg, d, ff),
      w2.reshape(N_GROUPS, epg, ff, d))


def _moe(x, g, w_grp, b_grp, w_exp, b_exp, w1, w3, w2):
    n, d = x.shape
    pad = LANES - N_GROUPS - N_EXPERTS
    wrt = jnp.concatenate([w_grp, w_exp, jnp.zeros((d, pad), F32)], axis=1).T
    brt = jnp.concatenate([b_grp, b_exp, jnp.zeros((pad,), F32)]).reshape(LANES, 1)
    toks, idx, cnt = _router(x, g, wrt, brt, MOE_ROWS)

    tb = MOE_BLOCK
    counts = cnt[0:N_GROUPS, 0].astype(jnp.int32)
    padded = (counts + tb - 1) // tb * tb
    ends = jnp.cumsum(padded)
    offs = ends - padded
    gsel = idx[:, 0, :].reshape(n).astype(jnp.int32)
    group_ids = jnp.arange(N_GROUPS, dtype=jnp.int32)
    dest = (jnp.sum(jnp.where(gsel[:, None] == group_ids[None, :], offs[None, :], 0), axis=1)
            + idx[:, 1, :].reshape(n).astype(jnp.int32))
    nblk_max = n // tb + N_GROUPS
    blk_start = jnp.arange(nblk_max, dtype=jnp.int32) * tb
    blk_grp = jnp.minimum(jnp.sum(blk_start[:, None] >= ends[None, :], axis=1), N_GROUPS - 1).astype(jnp.int32)
    nblk = (ends[N_GROUPS - 1:] // tb).astype(jnp.int32)

    sorted_toks = _permute_tokens(dest, toks, nblk_max * tb, True,
                                  init=jnp.zeros((nblk_max * tb * TOKEN_SUBLANES, LANES), jnp.uint32))
    ys = _experts(sorted_toks, blk_grp, nblk, w1, w3, w2)
    return _permute_tokens(dest, ys, n, False)


def _final_norm_kernel(x_ref, y_ref, g_ref, o_ref):
    o_ref[...] = _rms(x_ref[...] + _from_token_tiles(y_ref), g_ref[...])


def _final_norm(x, y, g, tm):
    n, d = x.shape
    row = pl.BlockSpec((tm, d), lambda i: (i, 0))
    return pl.pallas_call(
        _final_norm_kernel,
        out_shape=jax.ShapeDtypeStruct((n, d), F32),
        grid=(n // tm,),
        in_specs=[row, pl.BlockSpec((tm * TOKEN_SUBLANES, LANES), lambda i: (i, 0)),
                  pl.BlockSpec((1, d), lambda i: (0, 0))],
        out_specs=row,
        compiler_params=_params("parallel"),
        name="final_norm",
    )(x, y, g.reshape(1, d))


def kernel(x, mem, rel_bias, mem_norm_g, final_norm_g, hgrn_lb, norm_mix_g, norm_xattn_g, norm_ffn_g, even_w_in, even_w_out, hgrn_norm_g, conv_dw, conv_db, conv_ln_g, conv_ln_b, odd_w_qkv, odd_w_out, lam_q1, lam_k1, lam_q2, lam_k2, subln_g, xa_wq, xa_wkv, xa_wo, moe_w_grp, moe_b_grp, moe_w_exp, moe_b_exp, moe_w1, moe_w3, moe_w2):
    batch, seq, d = x.shape
    mem_len = mem.shape[1]
    depth = norm_mix_g.shape[0]
    n = batch * seq
    assert seq % ATT_TILE == 0 and seq % HGRN_ROWS == 0 and seq % CONV_ROWS == 0 and seq % ROW_TILE == 0
    assert n % MOE_ROWS == 0 and n % MOE_BLOCK == 0 and n % PERMUTE_ROWS == 0
    assert (batch * mem_len) % ROW_TILE == 0

    xf = x.reshape(n, d)
    memf = mem.reshape(batch * mem_len, d)
    bias_tiles = _bias_tiles(rel_bias)
    y_moe = None

    for i in range(depth):
        j = i // 2
        w_in = (even_w_in if i % 2 == 0 else odd_w_qkv)[j].astype(BF16)
        if y_moe is None:
            proj = _norm_matmul(xf, norm_mix_g[i], w_in, ROW_TILE)
        else:
            xf, proj = _add_norm_matmul(xf, y_moe, norm_mix_g[i], w_in, ROW_TILE)
        if i % 2 == 0:
            o_hgrn = _hgrn(proj, hgrn_lb, hgrn_norm_g[j], j, batch, seq)
            o_conv = _conv(proj, conv_dw[j], conv_db[j], conv_ln_g[j], conv_ln_b[j], batch, seq)
            xf = _resid_matmul(xf, [o_hgrn, o_conv], even_w_out[j].astype(BF16), ROW_TILE)
        else:
            lam_init = 0.8 - 0.6 * math.exp(-0.3 * i)
            lamv = jnp.stack([lam_q1[j], lam_k1[j], lam_q2[j], lam_k2[j]]).astype(F32)
            o_att = _diff_attn(proj, bias_tiles, lamv, subln_g[j], lam_init, batch, seq)
            xf = _resid_matmul(xf, [o_att], odd_w_out[j].astype(BF16), ROW_TILE)

        kv = _norm_matmul(memf, mem_norm_g, xa_wkv[i].astype(BF16), ROW_TILE)
        xf = _xattn(xf, norm_xattn_g[i], xa_wq[i].astype(BF16), kv, xa_wo[i].astype(BF16), seq, mem_len, ROW_TILE)

        y_moe = _moe(xf, norm_ffn_g[i], moe_w_grp[i], moe_b_grp[i], moe_w_exp[i], moe_b_exp[i],
                     moe_w1[i].astype(BF16), moe_w3[i].astype(BF16), moe_w2[i].astype(BF16))

    return _final_norm(xf, y_moe, final_norm_g, ROW_TILE).reshape(batch, seq, d)
```

```python
import functools
import math

import jax
import jax.numpy as jnp
from jax import lax
from jax.experimental import pallas as pl
from jax.experimental.pallas import tpu as pltpu

F32 = jnp.float32
BF16 = jnp.bfloat16

EPS = 1e-6
NEG_BIG = -1e30
LOG2E = math.log2(math.e)
TINY = 1e-30

HGRN_HEADS = 4
HGRN_DK = 128
HGRN_WIDTH = HGRN_HEADS * HGRN_DK
CONV_WIDTH = 31
DA_HEADS = 8
DA_HEAD_DIM = 64
DA_VDIM = 2 * DA_HEAD_DIM
REL_BUCKETS = 32
REL_MAX_DIST = 128
XA_HEADS = 4
N_GROUPS = 4
EXPERTS_PER_GROUP = 4
N_EXPERTS = N_GROUPS * EXPERTS_PER_GROUP

LANES = 128
TOKEN_SUBLANES = 8
V7X_VMEM_LIMIT_BYTES = 56 * 1024 * 1024

ROW_TILE = 512
HGRN_CHUNK = 128
HGRN_SUB = 16
HGRN_ROWS = 512
CONV_ROWS = 512
CONV_HALO = 32
ATT_TILE = 512
ATT_STRIP = 256
MOE_ROWS = 1024
ROUTE_ROWS = 24
MOE_BLOCK = 1024
PERMUTE_ROWS = 1024
PERMUTE_UNROLL = 8


def _params(*sem):
    return pltpu.CompilerParams(dimension_semantics=sem, vmem_limit_bytes=V7X_VMEM_LIMIT_BYTES)


def _rms(x, g):
    return x * lax.rsqrt(jnp.mean(x * x, axis=-1, keepdims=True) + EPS) * g


def _dot(a, b):
    return jnp.dot(a, b, preferred_element_type=F32)


def _dot_nt(a, b):
    return lax.dot_general(a, b, (((1,), (1,)), ((), ())), preferred_element_type=F32)


def _token_rows(s, n):
    return (pl.ds(s, n, stride=TOKEN_SUBLANES), slice(None))


def _from_token_tiles(y_ref):
    n = y_ref.shape[0] // TOKEN_SUBLANES
    return jnp.concatenate([y_ref[_token_rows(s, n)] for s in range(TOKEN_SUBLANES)], axis=1)


def _norm_matmul_kernel(x_ref, g_ref, w_ref, o_ref):
    h = _rms(x_ref[...], g_ref[...]).astype(BF16)
    o_ref[...] = _dot(h, w_ref[...]).astype(o_ref.dtype)


def _norm_matmul(x, g, w, tm):
    n, d = x.shape
    nout = w.shape[1]
    return pl.pallas_call(
        _norm_matmul_kernel,
        out_shape=jax.ShapeDtypeStruct((n, nout), BF16),
        grid=(n // tm,),
        in_specs=[pl.BlockSpec((tm, d), lambda i: (i, 0)),
                  pl.BlockSpec((1, d), lambda i: (0, 0)),
                  pl.BlockSpec((d, nout), lambda i: (0, 0))],
        out_specs=pl.BlockSpec((tm, nout), lambda i: (i, 0)),
        compiler_params=_params("parallel"),
        name="norm_matmul",
    )(x, g.reshape(1, d), w)


def _add_norm_matmul_kernel(x_ref, y_ref, g_ref, w_ref, xo_ref, o_ref):
    x = x_ref[...] + _from_token_tiles(y_ref)
    xo_ref[...] = x
    o_ref[...] = _dot(_rms(x, g_ref[...]).astype(BF16), w_ref[...]).astype(o_ref.dtype)


def _add_norm_matmul(x, y, g, w, tm):
    n, d = x.shape
    nout = w.shape[1]
    row = pl.BlockSpec((tm, d), lambda i: (i, 0))
    tiles = pl.BlockSpec((tm * TOKEN_SUBLANES, LANES), lambda i: (i, 0))
    return pl.pallas_call(
        _add_norm_matmul_kernel,
        out_shape=(jax.ShapeDtypeStruct((n, d), F32), jax.ShapeDtypeStruct((n, nout), BF16)),
        grid=(n // tm,),
        in_specs=[row, tiles, pl.BlockSpec((1, d), lambda i: (0, 0)), pl.BlockSpec((d, nout), lambda i: (0, 0))],
        out_specs=(row, pl.BlockSpec((tm, nout), lambda i: (i, 0))),
        compiler_params=_params("parallel"),
        name="add_norm_matmul",
    )(x, y, g.reshape(1, d), w)


def _resid_matmul_kernel(*refs):
    x_ref, a_refs, w_ref, o_ref = refs[0], refs[1:-2], refs[-2], refs[-1]
    a = [r[...] for r in a_refs]
    a = a[0] if len(a) == 1 else jnp.concatenate(a, axis=1)
    o_ref[...] = x_ref[...] + _dot(a, w_ref[...])


def _resid_matmul(x, acts, w, tm):
    n, d = x.shape
    return pl.pallas_call(
        _resid_matmul_kernel,
        out_shape=jax.ShapeDtypeStruct((n, d), F32),
        grid=(n // tm,),
        in_specs=[pl.BlockSpec((tm, d), lambda i: (i, 0))]
        + [pl.BlockSpec((tm, a.shape[1]), lambda i: (i, 0)) for a in acts]
        + [pl.BlockSpec(w.shape, lambda i: (0, 0))],
        out_specs=pl.BlockSpec((tm, d), lambda i: (i, 0)),
        compiler_params=_params("parallel"),
        name="resid_matmul",
    )(x, *acts, w)


def _cumsum_rows(g):
    rows = lax.broadcasted_iota(jnp.int32, g.shape, 0)
    b, sh = g, 1
    while sh < g.shape[0]:
        b = b + jnp.where(rows >= sh, pltpu.roll(b, sh, 0), 0.0)
        sh *= 2
    return b


def _hgrn_chunk(zq, zf, v, lb, st):
    c = zq.shape[0]
    f = lb + (1.0 - lb) * jax.nn.sigmoid(zf)
    g = jnp.log(jnp.maximum(f, TINY))
    kk = 1.0 - f
    q = zq * jax.nn.sigmoid(zq)
    b = _cumsum_rows(g)

    o = _dot_nt((q * jnp.exp(b)).astype(BF16), st.astype(BF16))
    b_last = b[c - 1:c]
    kdec = (kk * jnp.exp(b_last - b)).astype(BF16)
    st_new = st * jnp.exp(b_last) + _dot(v.T.astype(BF16), kdec)

    rows = lax.broadcasted_iota(jnp.int32, (HGRN_SUB, 1), 0)
    half = HGRN_SUB // 2
    a_rows, d_rows = [], []
    for i in range(c // HGRN_SUB):
        r0 = i * HGRN_SUB
        bi, qi, ki, vi = (t[r0:r0 + HGRN_SUB] for t in (b, q, kk, v))
        if i == 0:
            a_rows.append(jnp.zeros((HGRN_SUB, c), F32))
        else:
            beta = b[r0 - 1:r0]
            qs = (qi * jnp.exp(bi - beta)).astype(BF16)
            ks = (kk[:r0] * jnp.exp(beta - b[:r0])).astype(BF16)
            ks = jnp.concatenate([ks, jnp.zeros((c - r0, ks.shape[1]), BF16)], axis=0)
            a_rows.append(_dot_nt(qs, ks))
        top = jnp.zeros((half, v.shape[1]), F32)
        bot = jnp.zeros((half, v.shape[1]), F32)
        for s in range(HGRN_SUB):
            lo = 0 if s < half else half
            w = qi[lo:] * jnp.exp(jnp.minimum(bi[lo:] - bi[s:s + 1], 0.0)) * ki[s:s + 1]
            rs = jnp.where(rows[lo:] >= s, jnp.sum(w, axis=1, keepdims=True), 0.0)
            contrib = rs * vi[s:s + 1]
            if lo == 0:
                top = top + contrib[:half]
                bot = bot + contrib[half:]
            else:
                bot = bot + contrib
        d_rows += [top, bot]
    a = jnp.concatenate(a_rows, axis=0).astype(BF16)
    o = o + _dot(a, v.astype(BF16)) + jnp.concatenate(d_rows, axis=0)
    return o, st_new


def _hgrn_kernel(zq_ref, zf_ref, zi_ref, zg_ref, lbp_ref, gn_ref, o_ref, st_ref, *, layer):
    @pl.when(pl.program_id(2) == 0)
    def _():
        st_ref[...] = jnp.zeros_like(st_ref)

    lbp = lbp_ref[...]
    e = jnp.exp(lbp - jnp.max(lbp, axis=0, keepdims=True))
    p = e / jnp.sum(e, axis=0, keepdims=True)
    lb = jnp.zeros((1, lbp.shape[1]), F32)
    for r in range(1, layer + 1):
        lb = lb + p[r:r + 1]
    gn = gn_ref[...]

    def chunk(ci, carry):
        r0 = pl.multiple_of(ci * HGRN_CHUNK, HGRN_CHUNK)
        sl = pl.ds(r0, HGRN_CHUNK)
        o, st_new = _hgrn_chunk(zq_ref[sl, :].astype(F32), zf_ref[sl, :].astype(F32),
                                zi_ref[sl, :].astype(F32), lb, st_ref[...])
        st_ref[...] = st_new
        o = _rms(o, gn) * jax.nn.sigmoid(zg_ref[sl, :].astype(F32))
        o_ref[sl, :] = o.astype(o_ref.dtype)
        return carry

    lax.fori_loop(0, o_ref.shape[0] // HGRN_CHUNK, chunk, 0)


def _hgrn(proj, hgrn_lb, gn, layer, batch, seq):
    n = proj.shape[0]
    ns = seq // HGRN_ROWS
    hh = HGRN_HEADS

    def col(k):
        return pl.BlockSpec((HGRN_ROWS, HGRN_DK), lambda b, h, s: (b * ns + s, k * hh + h))

    return pl.pallas_call(
        functools.partial(_hgrn_kernel, layer=layer),
        out_shape=jax.ShapeDtypeStruct((n, HGRN_WIDTH), BF16),
        grid=(batch, hh, ns),
        in_specs=[col(0), col(1), col(2), col(3),
                  pl.BlockSpec((hgrn_lb.shape[0], HGRN_DK), lambda b, h, s: (0, h)),
                  pl.BlockSpec((1, HGRN_DK), lambda b, h, s: (0, 0))],
        out_specs=pl.BlockSpec((HGRN_ROWS, HGRN_DK), lambda b, h, s: (b * ns + s, h)),
        scratch_shapes=[pltpu.VMEM((HGRN_DK, HGRN_DK), F32)],
        compiler_params=_params("parallel", "parallel", "arbitrary"),
        name="hgrn2",
    )(proj, proj, proj, proj, hgrn_lb, gn.reshape(1, HGRN_DK))


def _conv_kernel(cu_ref, cv_ref, dw_ref, db_ref, g_ref, b_ref, o_ref, ext_ref, rot_ref):
    rows = o_ref.shape[0]

    @pl.when(pl.program_id(1) == 0)
    def _():
        ext_ref[0:CONV_HALO, :] = jnp.zeros((CONV_HALO, ext_ref.shape[1]), F32)

    @pl.when(pl.program_id(1) > 0)
    def _():
        ext_ref[0:CONV_HALO, :] = ext_ref[rows:rows + CONV_HALO, :]

    ext_ref[CONV_HALO:, :] = cu_ref[...].astype(F32) * jax.nn.sigmoid(cv_ref[...].astype(F32))
    span = rot_ref.shape[1]
    for r in range(1, TOKEN_SUBLANES):
        rot_ref[r - 1] = ext_ref[r:r + span, :]
    acc = jnp.zeros(o_ref.shape, F32) + db_ref[...]
    first = CONV_HALO - (CONV_WIDTH - 1)
    for j in range(CONV_WIDTH):
        a, r = divmod(first + j, TOKEN_SUBLANES)
        a *= TOKEN_SUBLANES
        tap = ext_ref[a:a + rows, :] if r == 0 else rot_ref[r - 1, a:a + rows, :]
        acc = acc + dw_ref[j:j + 1, :] * tap
    mu = jnp.mean(acc, axis=-1, keepdims=True)
    cen = acc - mu
    var = jnp.mean(cen * cen, axis=-1, keepdims=True)
    y = cen * lax.rsqrt(var + EPS) * g_ref[...] + b_ref[...]
    o_ref[...] = (y * jax.nn.sigmoid(y)).astype(o_ref.dtype)


def _conv(proj, dw, db, ln_g, ln_b, batch, seq):
    n = proj.shape[0]
    ch = dw.shape[1]
    ns = seq // CONV_ROWS
    first_blk = 4 * HGRN_WIDTH // ch
    vec = pl.BlockSpec((1, ch), lambda b, s: (0, 0))
    return pl.pallas_call(
        _conv_kernel,
        out_shape=jax.ShapeDtypeStruct((n, ch), BF16),
        grid=(batch, ns),
        in_specs=[pl.BlockSpec((CONV_ROWS, ch), lambda b, s: (b * ns + s, first_blk)),
                  pl.BlockSpec((CONV_ROWS, ch), lambda b, s: (b * ns + s, first_blk + 1)),
                  pl.BlockSpec((CONV_WIDTH, ch), lambda b, s: (0, 0)), vec, vec, vec],
        out_specs=pl.BlockSpec((CONV_ROWS, ch), lambda b, s: (b * ns + s, 0)),
        scratch_shapes=[pltpu.VMEM((CONV_ROWS + CONV_HALO, ch), F32),
                        pltpu.VMEM((TOKEN_SUBLANES - 1, CONV_ROWS + CONV_HALO - TOKEN_SUBLANES, ch), F32)],
        compiler_params=_params("parallel", "arbitrary"),
        name="glu_conv_ln",
    )(proj, proj, dw, db.reshape(1, ch), ln_g.reshape(1, ch), ln_b.reshape(1, ch))


def _bias_kernel(tab_ref, o_ref):
    h = pl.program_id(0)
    t = o_ref.shape[2]
    k = lax.broadcasted_iota(jnp.int32, (t, t), 0)
    q = lax.broadcasted_iota(jnp.int32, (t, t), 1)
    max_exact = REL_BUCKETS // 2
    for d in range(2):
        rel = q - k + d * t
        nn = jnp.maximum(rel, 0)
        nf = jnp.maximum(nn, 1).astype(F32)
        large = max_exact + (jnp.log(nf / max_exact) / math.log(REL_MAX_DIST / max_exact)
                             * (REL_BUCKETS - max_exact)).astype(jnp.int32)
        large = jnp.clip(large, 0, REL_BUCKETS - 1)
        bucket = jnp.where(nn < max_exact, nn, large)
        val = jnp.zeros((t, t), F32)
        for bk in range(REL_BUCKETS):
            val = jnp.where(bucket == bk, tab_ref[bk, h], val)
        val = (val - tab_ref[REL_BUCKETS - 1, h]) * LOG2E
        if d == 0:
            val = jnp.where(rel >= 0, val, NEG_BIG)
        o_ref[0, d] = val
    o_ref[0, 2] = jnp.zeros((t, t), F32)


def _bias_tiles(rel_bias):
    assert ATT_TILE >= REL_MAX_DIST
    heads = rel_bias.shape[1]
    return pl.pallas_call(
        _bias_kernel,
        out_shape=jax.ShapeDtypeStruct((heads, 3, ATT_TILE, ATT_TILE), F32),
        grid=(heads,),
        in_specs=[pl.BlockSpec(memory_space=pltpu.SMEM)],
        out_specs=pl.BlockSpec((1, 3, ATT_TILE, ATT_TILE), lambda h: (h, 0, 0, 0)),
        compiler_params=_params("parallel"),
        name="t5_bias_tiles",
    )(rel_bias)


def _attn_kernel(q_ref, k_ref, v_ref, bias_ref, lamv_ref, g_ref, o_ref,
                 vt_ref, qs_ref, m_ref, l_ref, acc_ref, sa_ref, sb_ref, cma_ref, cmb_ref, *, lam_init):
    i = pl.program_id(2)
    t = o_ref.shape[0]

    @pl.when(i == 0)
    def _():
        for j in range(vt_ref.shape[0]):
            vt_ref[j] = v_ref[j * t:(j + 1) * t, :].astype(F32).T.astype(BF16)

    lane = lax.broadcasted_iota(jnp.int32, (1, DA_VDIM), 1)
    q = q_ref[...].astype(F32) * (DA_HEAD_DIM ** -0.5 * LOG2E)
    qs_ref[0] = jnp.where(lane < DA_HEAD_DIM, q, 0.0).astype(BF16)
    qs_ref[1] = jnp.where(lane >= DA_HEAD_DIM, q, 0.0).astype(BF16)
    m_ref[...] = jnp.full(m_ref.shape, -jnp.inf, F32)
    l_ref[...] = jnp.zeros(l_ref.shape, F32)
    acc_ref[...] = jnp.zeros(acc_ref.shape, F32)

    def produce(j, s_dst, cm_dst, far=False):
        kb = k_ref[pl.ds(pl.multiple_of(j * t, t), t), :]
        bt = None if far else bias_ref[0, jnp.minimum(i - j, 2)]
        for mp in range(2):
            s = _dot_nt(kb, qs_ref[mp])
            s = s if far else s + bt
            s_dst[mp] = s
            cm_dst[mp] = jnp.max(s, axis=0, keepdims=True)

    def consume(j, s_src, cm_src):
        vt = vt_ref[j]
        for mp in range(2):
            for st in range(t // ATT_STRIP):
                cols = slice(st * ATT_STRIP, (st + 1) * ATT_STRIP)
                m_old = m_ref[mp, :, cols]
                mn = jnp.maximum(m_old, cm_src[mp, :, cols])
                p = jnp.exp2(s_src[mp, :, cols] - mn)
                al = jnp.exp2(m_old - mn)
                m_ref[mp, :, cols] = mn
                l_ref[mp, :, cols] = l_ref[mp, :, cols] * al + jnp.sum(p, axis=0, keepdims=True)
                acc_ref[mp, :, cols] = acc_ref[mp, :, cols] * al + _dot(vt, p.astype(BF16))

    produce(0, sa_ref, cma_ref)

    def pair(pi, carry, far):
        j = 2 * pi
        produce(j + 1, sb_ref, cmb_ref, far)
        consume(j, sa_ref, cma_ref)
        produce(j + 2, sa_ref, cma_ref, far)
        consume(j + 1, sb_ref, cmb_ref)
        return carry

    n_far = jnp.maximum(i - 2, 0) >> 1
    lax.fori_loop(0, n_far, functools.partial(pair, far=True), 0)
    lax.fori_loop(n_far, i >> 1, functools.partial(pair, far=False), 0)

    @pl.when((i & 1) == 0)
    def _():
        consume(i, sa_ref, cma_ref)

    @pl.when((i & 1) == 1)
    def _():
        produce(i, sb_ref, cmb_ref)
        consume(i - 1, sa_ref, cma_ref)
        consume(i, sb_ref, cmb_ref)

    lv = lamv_ref[...]
    lam = (jnp.exp(jnp.sum(lv[0:1] * lv[1:2], axis=1, keepdims=True))
           - jnp.exp(jnp.sum(lv[2:3] * lv[3:4], axis=1, keepdims=True)) + lam_init)
    o = acc_ref[0] * (1.0 / l_ref[0]) - lam * (acc_ref[1] * (1.0 / l_ref[1]))
    y = o * lax.rsqrt(jnp.mean(o * o, axis=0, keepdims=True) + EPS) * g_ref[...] * (1.0 - lam_init)
    o_ref[...] = y.T.astype(o_ref.dtype)


def _diff_attn(qkv, bias_tiles, lamv, subln_g, lam_init, batch, seq):
    n = qkv.shape[0]
    t = ATT_TILE
    nq = seq // t
    hh = DA_HEADS
    return pl.pallas_call(
        functools.partial(_attn_kernel, lam_init=lam_init),
        out_shape=jax.ShapeDtypeStruct((n, hh * DA_VDIM), BF16),
        grid=(batch, hh, nq),
        in_specs=[pl.BlockSpec((t, DA_VDIM), lambda b, h, i: (b * nq + i, h)),
                  pl.BlockSpec((seq, DA_VDIM), lambda b, h, i: (b, hh + h)),
                  pl.BlockSpec((seq, DA_VDIM), lambda b, h, i: (b, 2 * hh + h)),
                  pl.BlockSpec((1, 3, t, t), lambda b, h, i: (h, 0, 0, 0)),
                  pl.BlockSpec(lamv.shape, lambda b, h, i: (0, 0)),
                  pl.BlockSpec((DA_VDIM, 1), lambda b, h, i: (0, 0))],
        out_specs=pl.BlockSpec((t, DA_VDIM), lambda b, h, i: (b * nq + i, h)),
        scratch_shapes=[pltpu.VMEM((nq, DA_VDIM, t), BF16), pltpu.VMEM((2, t, DA_VDIM), BF16),
                        pltpu.VMEM((2, 1, t), F32), pltpu.VMEM((2, 1, t), F32),
                        pltpu.VMEM((2, DA_VDIM, t), F32),
                        pltpu.VMEM((2, t, t), F32), pltpu.VMEM((2, t, t), F32),
                        pltpu.VMEM((2, 1, t), F32), pltpu.VMEM((2, 1, t), F32)],
        compiler_params=_params("parallel", "parallel", "arbitrary"),
        name="diff_attention",
    )(qkv, qkv, qkv, bias_tiles, lamv, subln_g.reshape(DA_VDIM, 1))


def _xattn_kernel(x_ref, g_ref, wq_ref, kv_ref, wo_ref, o_ref):
    x = x_ref[...]
    d = x.shape[1]
    dh = d // XA_HEADS
    h = _rms(x, g_ref[...]).astype(BF16)
    q = (_dot(h, wq_ref[...]) * (dh ** -0.5)).astype(BF16)
    kv = kv_ref[...]
    outs = []
    for hd in range(XA_HEADS):
        s = _dot_nt(q[:, hd * dh:(hd + 1) * dh], kv[:, hd * dh:(hd + 1) * dh])
        p = jnp.exp(s - jnp.max(s, axis=-1, keepdims=True))
        pv = _dot(p.astype(BF16), kv[:, d + hd * dh:d + (hd + 1) * dh])
        outs.append(pv * (1.0 / jnp.sum(p, axis=-1, keepdims=True)))
    o = jnp.concatenate(outs, axis=1).astype(BF16)
    o_ref[...] = x + _dot(o, wo_ref[...])


def _xattn(x, g, wq, kv, wo, seq, mem_len, tm):
    n, d = x.shape
    per_batch = seq // tm
    return pl.pallas_call(
        _xattn_kernel,
        out_shape=jax.ShapeDtypeStruct((n, d), F32),
        grid=(n // tm,),
        in_specs=[pl.BlockSpec((tm, d), lambda i: (i, 0)),
                  pl.BlockSpec((1, d), lambda i: (0, 0)),
                  pl.BlockSpec((d, d), lambda i: (0, 0)),
                  pl.BlockSpec((mem_len, 2 * d), lambda i: (i // per_batch, 0)),
                  pl.BlockSpec((d, d), lambda i: (0, 0))],
        out_specs=pl.BlockSpec((tm, d), lambda i: (i, 0)),
        compiler_params=_params("parallel"),
        name="memory_xattn",
    )(x, g.reshape(1, d), wq, kv, wo)


def _route_t(lg):
    row = lax.broadcasted_iota(jnp.int32, lg.shape, 0)
    big = lg.shape[0]
    is_grp = row < N_GROUPS
    gl = jnp.where(is_grp, lg, -jnp.inf)
    gmax = jnp.max(gl, axis=0, keepdims=True)
    gsel = jnp.min(jnp.where(gl == gmax, row, big), axis=0, keepdims=True)
    g_w = 1.0 / jnp.sum(jnp.where(is_grp, jnp.exp(lg - gmax), 0.0), axis=0, keepdims=True)
    e_idx = row - N_GROUPS
    in_grp = (e_idx >= 0) & (e_idx < N_EXPERTS) & (e_idx // EXPERTS_PER_GROUP == gsel)
    ml = jnp.where(in_grp, lg, -jnp.inf)
    v1 = jnp.max(ml, axis=0, keepdims=True)
    i1 = jnp.min(jnp.where(ml == v1, row, big), axis=0, keepdims=True)
    ml2 = jnp.where(row == i1, -jnp.inf, ml)
    v2 = jnp.max(ml2, axis=0, keepdims=True)
    i2 = jnp.min(jnp.where(ml2 == v2, row, big), axis=0, keepdims=True)
    t = jnp.exp(v2 - v1)
    w1 = g_w / (1.0 + t)
    w2 = g_w * t / (1.0 + t)
    return gsel, jnp.where(row == i1, w1, 0.0) + jnp.where(row == i2, w2, 0.0)


def _router_kernel(x_ref, g_ref, wrt_ref, brt_ref, tok_ref, idx_ref, cnt_ref, tri_ref, run_ref):
    t, d = x_ref.shape

    @pl.when(pl.program_id(0) == 0)
    def _():
        r = lax.broadcasted_iota(jnp.int32, (t, t), 0)
        c = lax.broadcasted_iota(jnp.int32, (t, t), 1)
        tri_ref[...] = (r < c).astype(BF16)
        run_ref[...] = jnp.zeros_like(run_ref)

    h = _rms(x_ref[...], g_ref[...])
    logits = lax.dot_general(wrt_ref[...], h, (((1,), (1,)), ((), ())), preferred_element_type=F32,
                             precision=lax.Precision.HIGHEST) + brt_ref[...]
    gsel, gate = _route_t(logits[0:ROUTE_ROWS])

    row8 = lax.broadcasted_iota(jnp.int32, (8, t), 0)
    memb = (row8 == gsel).astype(F32)
    before = _dot(memb.astype(BF16), tri_ref[...]) + run_ref[...]
    pos = jnp.sum(memb * before, axis=0, keepdims=True)
    run_ref[...] = run_ref[...] + jnp.sum(memb, axis=1, keepdims=True)
    cnt_ref[...] = run_ref[:, 0:LANES]
    idx_ref[0] = jnp.concatenate([gsel.astype(F32), pos, jnp.zeros((6, t), F32)], axis=0)

    bits = lax.bitcast_convert_type(h.astype(BF16).astype(F32), jnp.uint32)
    words = (bits[:, 0:d // 2] >> 16) | bits[:, d // 2:]
    h_sublanes = d // 2 // LANES
    for s in range(h_sublanes):
        tok_ref[_token_rows(s, t)] = words[:, s * LANES:(s + 1) * LANES]
    g16 = gate[N_GROUPS:N_GROUPS + N_EXPERTS]
    gate_rows = jnp.concatenate([g16, jnp.zeros((LANES - N_EXPERTS, t), F32)], axis=0).T
    tok_ref[_token_rows(h_sublanes, t)] = lax.bitcast_convert_type(gate_rows, jnp.uint32)
    for s in range(h_sublanes + 1, TOKEN_SUBLANES):
        tok_ref[_token_rows(s, t)] = jnp.zeros((t, LANES), jnp.uint32)


def _router(x, g, wrt, brt, tm):
    n, d = x.shape
    nt = n // tm
    assert d // 2 // LANES + 1 <= TOKEN_SUBLANES
    return pl.pallas_call(
        _router_kernel,
        out_shape=(jax.ShapeDtypeStruct((n * TOKEN_SUBLANES, LANES), jnp.uint32),
                   jax.ShapeDtypeStruct((nt, 8, tm), F32),
                   jax.ShapeDtypeStruct((8, LANES), F32)),
        grid=(nt,),
        in_specs=[pl.BlockSpec((tm, d), lambda i: (i, 0)),
                  pl.BlockSpec((1, d), lambda i: (0, 0)),
                  pl.BlockSpec((LANES, d), lambda i: (0, 0)),
                  pl.BlockSpec((LANES, 1), lambda i: (0, 0))],
        out_specs=(pl.BlockSpec((tm * TOKEN_SUBLANES, LANES), lambda i: (i, 0)),
                   pl.BlockSpec((1, 8, tm), lambda i: (i, 0, 0)),
                   pl.BlockSpec((8, LANES), lambda i: (0, 0))),
        scratch_shapes=[pltpu.VMEM((tm, tm), BF16), pltpu.VMEM((8, tm), F32)],
        compiler_params=_params("arbitrary"),
        name="moe_router",
    )(x, g.reshape(1, d), wrt, brt)


def _permute_kernel(idx_ref, src_ref, *rest, scatter):
    dst_ref, sem = rest[-2], rest[-1]
    rows = idx_ref.shape[2]

    def copy(r):
        j = idx_ref[0, 0, r]
        src_row, dst_row = (r, j) if scatter else (j, r)
        return pltpu.make_async_copy(src_ref.at[pl.ds(src_row, 1)], dst_ref.at[pl.ds(dst_row, 1)], sem)

    def start(k, carry):
        for u in range(PERMUTE_UNROLL):
            copy(k * PERMUTE_UNROLL + u).start(priority=u % 2)
        return carry

    def wait(r, carry):
        copy(r).wait()
        return carry

    lax.fori_loop(0, rows // PERMUTE_UNROLL, start, 0)
    lax.fori_loop(0, rows, wait, 0, unroll=PERMUTE_UNROLL)


def _permute_tokens(idx, src, dst_tokens, scatter, init=None):
    n = idx.shape[0]
    rows = PERMUTE_ROWS
    tile = (TOKEN_SUBLANES, LANES)
    any_spec = pl.BlockSpec(memory_space=pl.ANY)
    block = pl.BlockSpec((rows, *tile), lambda i: (i, 0, 0))
    ins = [idx.reshape(n // rows, 1, rows), src.reshape(-1, *tile)]
    if init is not None:
        ins.append(init.reshape(-1, *tile))
    out = pl.pallas_call(
        functools.partial(_permute_kernel, scatter=scatter),
        out_shape=jax.ShapeDtypeStruct((dst_tokens, *tile), src.dtype),
        grid=(n // rows,),
        in_specs=[pl.BlockSpec((1, 1, rows), lambda i: (i, 0, 0), memory_space=pltpu.SMEM),
                  block if scatter else any_spec] + ([any_spec] if init is not None else []),
        out_specs=any_spec if scatter else block,
        scratch_shapes=[pltpu.SemaphoreType.DMA],
        input_output_aliases={2: 0} if init is not None else {},
        compiler_params=_params("arbitrary"),
        name="moe_scatter_tokens" if scatter else "moe_gather_tokens",
    )(*ins)
    return out.reshape(dst_tokens * TOKEN_SUBLANES, LANES)


def _experts_kernel(grp_ref, nblk_ref, tok_ref, w1_ref, w3_ref, w2_ref, o_ref):
    i = pl.program_id(0)
    tb = o_ref.shape[0] // TOKEN_SUBLANES
    d = w1_ref.shape[2]
    h_sublanes = d // 2 // LANES

    @pl.when(i < nblk_ref[0])
    def _():
        words = [tok_ref[_token_rows(s, tb)] for s in range(h_sublanes)]
        low = [lax.bitcast_convert_type(w << 16, F32).astype(BF16) for w in words]
        high = [lax.bitcast_convert_type(w & jnp.uint32(0xFFFF0000), F32).astype(BF16) for w in words]
        h = jnp.concatenate(low + high, axis=1)
        gates = lax.bitcast_convert_type(tok_ref[_token_rows(h_sublanes, tb)], F32)
        lane = lax.broadcasted_iota(jnp.int32, (1, LANES), 1)
        first = grp_ref[i] * EXPERTS_PER_GROUP
        acc = jnp.zeros((tb, d), F32)
        for e in range(EXPERTS_PER_GROUP):
            gate = jnp.sum(jnp.where(lane == first + e, gates, 0.0), axis=1, keepdims=True)
            h1 = _dot(h, w1_ref[0, e])
            hid = h1 * jax.nn.sigmoid(h1) * _dot(h, w3_ref[0, e]) * gate
            acc = acc + _dot(hid.astype(BF16), w2_ref[0, e])
        for s in range(TOKEN_SUBLANES):
            o_ref[_token_rows(s, tb)] = acc[:, s * LANES:(s + 1) * LANES]

    @pl.when(i >= nblk_ref[0])
    def _():
        o_ref[...] = jnp.zeros_like(o_ref)


def _experts(toks, blk_grp, nblk, w1, w3, w2):
    d, ff = w1.shape[-2:]
    assert d == TOKEN_SUBLANES * LANES
    tb = MOE_BLOCK
    epg = EXPERTS_PER_GROUP
    tile_rows = pl.BlockSpec((tb * TOKEN_SUBLANES, LANES), lambda i, grp, nb: (i, 0))
    grid_spec = pltpu.PrefetchScalarGridSpec(
        num_scalar_prefetch=2,
        grid=(toks.shape[0] // (tb * TOKEN_SUBLANES),),
        in_specs=[tile_rows,
                  pl.BlockSpec((1, epg, d, ff), lambda i, grp, nb: (grp[i], 0, 0, 0)),
                  pl.BlockSpec((1, epg, d, ff), lambda i, grp, nb: (grp[i], 0, 0, 0)),
                  pl.BlockSpec((1, epg, ff, d), lambda i, grp, nb: (grp[i], 0, 0, 0))],
        out_specs=tile_rows,
    )
    return pl.pallas_call(
        _experts_kernel,
        out_shape=jax.ShapeDtypeStruct(toks.shape, F32),
        grid_spec=grid_spec,
        compiler_params=_params("arbitrary"),
        name="moe_experts",
    )(blk_grp, nblk, toks, w1.reshape(N_GROUPS, epg, d, ff), w3.reshape(N_GROUPS, epg, d, ff),
      w2.reshape(N_GROUPS, epg, ff, d))


def _moe(x, g, w_grp, b_grp, w_exp, b_exp, w1, w3, w2):
    n, d = x.shape
    pad = LANES - N_GROUPS - N_EXPERTS
    wrt = jnp.concatenate([w_grp, w_exp, jnp.zeros((d, pad), F32)], axis=1).T
    brt = jnp.concatenate([b_grp, b_exp, jnp.zeros((pad,), F32)]).reshape(LANES, 1)
    toks, idx, cnt = _router(x, g, wrt, brt, MOE_ROWS)

    tb = MOE_BLOCK
    counts = cnt[0:N_GROUPS, 0].astype(jnp.int32)
    padded = (counts + tb - 1) // tb * tb
    ends = jnp.cumsum(padded)
    offs = ends - padded
    gsel = idx[:, 0, :].reshape(n).astype(jnp.int32)
    group_ids = jnp.arange(N_GROUPS, dtype=jnp.int32)
    dest = (jnp.sum(jnp.where(gsel[:, None] == group_ids[None, :], offs[None, :], 0), axis=1)
            + idx[:, 1, :].reshape(n).astype(jnp.int32))
    nblk_max = n // tb + N_GROUPS
    blk_start = jnp.arange(nblk_max, dtype=jnp.int32) * tb
    blk_grp = jnp.minimum(jnp.sum(blk_start[:, None] >= ends[None, :], axis=1), N_GROUPS - 1).astype(jnp.int32)
    nblk = (ends[N_GROUPS - 1:] // tb).astype(jnp.int32)

    sorted_toks = _permute_tokens(dest, toks, nblk_max * tb, True,
                                  init=jnp.zeros((nblk_max * tb * TOKEN_SUBLANES, LANES), jnp.uint32))
    ys = _experts(sorted_toks, blk_grp, nblk, w1, w3, w2)
    return _permute_tokens(dest, ys, n, False)


def _final_norm_kernel(x_ref, y_ref, g_ref, o_ref):
    o_ref[...] = _rms(x_ref[...] + _from_token_tiles(y_ref), g_ref[...])


def _final_norm(x, y, g, tm):
    n, d = x.shape
    row = pl.BlockSpec((tm, d), lambda i: (i, 0))
    return pl.pallas_call(
        _final_norm_kernel,
        out_shape=jax.ShapeDtypeStruct((n, d), F32),
        grid=(n // tm,),
        in_specs=[row, pl.BlockSpec((tm * TOKEN_SUBLANES, LANES), lambda i: (i, 0)),
                  pl.BlockSpec((1, d), lambda i: (0, 0))],
        out_specs=row,
        compiler_params=_params("parallel"),
        name="final_norm",
    )(x, y, g.reshape(1, d))


def kernel(x, mem, rel_bias, mem_norm_g, final_norm_g, hgrn_lb, norm_mix_g, norm_xattn_g, norm_ffn_g, even_w_in, even_w_out, hgrn_norm_g, conv_dw, conv_db, conv_ln_g, conv_ln_b, odd_w_qkv, odd_w_out, lam_q1, lam_k1, lam_q2, lam_k2, subln_g, xa_wq, xa_wkv, xa_wo, moe_w_grp, moe_b_grp, moe_w_exp, moe_b_exp, moe_w1, moe_w3, moe_w2):
    batch, seq, d = x.shape
    mem_len = mem.shape[1]
    depth = norm_mix_g.shape[0]
    n = batch * seq
    assert seq % ATT_TILE == 0 and seq % HGRN_ROWS == 0 and seq % CONV_ROWS == 0 and seq % ROW_TILE == 0
    assert n % MOE_ROWS == 0 and n % MOE_BLOCK == 0 and n % PERMUTE_ROWS == 0
    assert (batch * mem_len) % ROW_TILE == 0

    xf = x.reshape(n, d)
    memf = mem.reshape(batch * mem_len, d)
    bias_tiles = _bias_tiles(rel_bias)
    y_moe = None

    for i in range(depth):
        j = i // 2
        w_in = (even_w_in if i % 2 == 0 else odd_w_qkv)[j].astype(BF16)
        if y_moe is None:
            proj = _norm_matmul(xf, norm_mix_g[i], w_in, ROW_TILE)
        else:
            xf, proj = _add_norm_matmul(xf, y_moe, norm_mix_g[i], w_in, ROW_TILE)
        if i % 2 == 0:
            o_hgrn = _hgrn(proj, hgrn_lb, hgrn_norm_g[j], j, batch, seq)
            o_conv = _conv(proj, conv_dw[j], conv_db[j], conv_ln_g[j], conv_ln_b[j], batch, seq)
            xf = _resid_matmul(xf, [o_hgrn, o_conv], even_w_out[j].astype(BF16), ROW_TILE)
        else:
            lam_init = 0.8 - 0.6 * math.exp(-0.3 * i)
            lamv = jnp.stack([lam_q1[j], lam_k1[j], lam_q2[j], lam_k2[j]]).astype(F32)
            o_att = _diff_attn(proj, bias_tiles, lamv, subln_g[j], lam_init, batch, seq)
            xf = _resid_matmul(xf, [o_att], odd_w_out[j].astype(BF16), ROW_TILE)

        kv = _norm_matmul(memf, mem_norm_g, xa_wkv[i].astype(BF16), ROW_TILE)
        xf = _xattn(xf, norm_xattn_g[i], xa_wq[i].astype(BF16), kv, xa_wo[i].astype(BF16), seq, mem_len, ROW_TILE)

        y_moe = _moe(xf, norm_ffn_g[i], moe_w_grp[i], moe_b_grp[i], moe_w_exp[i], moe_b_exp[i],
                     moe_w1[i].astype(BF16), moe_w3[i].astype(BF16), moe_w2[i].astype(BF16))

    return _final_norm(xf, y_moe, final_norm_g, ROW_TILE).reshape(batch, seq, d)
```

```python
import functools
import math

import jax
import jax.numpy as jnp
from jax import lax
from jax.experimental import pallas as pl
from jax.experimental.pallas import tpu as pltpu

F32 = jnp.float32
BF16 = jnp.bfloat16

EPS = 1e-6
NEG_BIG = -1e30
LOG2E = math.log2(math.e)
TINY = 1e-30

HGRN_HEADS = 4
HGRN_DK = 128
HGRN_WIDTH = HGRN_HEADS * HGRN_DK
CONV_WIDTH = 31
DA_HEADS = 8
DA_HEAD_DIM = 64
DA_VDIM = 2 * DA_HEAD_DIM
REL_BUCKETS = 32
REL_MAX_DIST = 128
XA_HEADS = 4
N_GROUPS = 4
EXPERTS_PER_GROUP = 4
N_EXPERTS = N_GROUPS * EXPERTS_PER_GROUP

LANES = 128
TOKEN_SUBLANES = 8
V7X_VMEM_LIMIT_BYTES = 56 * 1024 * 1024

ROW_TILE = 512
HGRN_CHUNK = 128
HGRN_SUB = 16
HGRN_HEADS_PER_STEP = 2
HGRN_ROWS = 512
CONV_ROWS = 512
CONV_HALO = 32
ATT_TILE = 512
ATT_STRIP = 256
MOE_ROWS = 1024
ROUTE_ROWS = 24
MOE_BLOCK = 1024
PERMUTE_ROWS = 1024
PERMUTE_UNROLL = 8


def _params(*sem):
    return pltpu.CompilerParams(dimension_semantics=sem, vmem_limit_bytes=V7X_VMEM_LIMIT_BYTES)


def _rms(x, g):
    return x * lax.rsqrt(jnp.mean(x * x, axis=-1, keepdims=True) + EPS) * g


def _dot(a, b):
    return jnp.dot(a, b, preferred_element_type=F32)


def _dot_nt(a, b):
    return lax.dot_general(a, b, (((1,), (1,)), ((), ())), preferred_element_type=F32)


def _token_rows(s, n):
    return (pl.ds(s, n, stride=TOKEN_SUBLANES), slice(None))


def _from_token_tiles(y_ref):
    n = y_ref.shape[0] // TOKEN_SUBLANES
    return jnp.concatenate([y_ref[_token_rows(s, n)] for s in range(TOKEN_SUBLANES)], axis=1)


def _norm_matmul_kernel(x_ref, g_ref, w_ref, o_ref):
    h = _rms(x_ref[...], g_ref[...]).astype(BF16)
    o_ref[...] = _dot(h, w_ref[...]).astype(o_ref.dtype)


def _norm_matmul(x, g, w, tm):
    n, d = x.shape
    nout = w.shape[1]
    return pl.pallas_call(
        _norm_matmul_kernel,
        out_shape=jax.ShapeDtypeStruct((n, nout), BF16),
        grid=(n // tm,),
        in_specs=[pl.BlockSpec((tm, d), lambda i: (i, 0)),
                  pl.BlockSpec((1, d), lambda i: (0, 0)),
                  pl.BlockSpec((d, nout), lambda i: (0, 0))],
        out_specs=pl.BlockSpec((tm, nout), lambda i: (i, 0)),
        compiler_params=_params("parallel"),
        name="norm_matmul",
    )(x, g.reshape(1, d), w)


def _add_norm_matmul_kernel(x_ref, y_ref, g_ref, w_ref, xo_ref, o_ref):
    x = x_ref[...] + _from_token_tiles(y_ref)
    xo_ref[...] = x
    o_ref[...] = _dot(_rms(x, g_ref[...]).astype(BF16), w_ref[...]).astype(o_ref.dtype)


def _add_norm_matmul(x, y, g, w, tm):
    n, d = x.shape
    nout = w.shape[1]
    row = pl.BlockSpec((tm, d), lambda i: (i, 0))
    tiles = pl.BlockSpec((tm * TOKEN_SUBLANES, LANES), lambda i: (i, 0))
    return pl.pallas_call(
        _add_norm_matmul_kernel,
        out_shape=(jax.ShapeDtypeStruct((n, d), F32), jax.ShapeDtypeStruct((n, nout), BF16)),
        grid=(n // tm,),
        in_specs=[row, tiles, pl.BlockSpec((1, d), lambda i: (0, 0)), pl.BlockSpec((d, nout), lambda i: (0, 0))],
        out_specs=(row, pl.BlockSpec((tm, nout), lambda i: (i, 0))),
        compiler_params=_params("parallel"),
        name="add_norm_matmul",
    )(x, y, g.reshape(1, d), w)


def _resid_matmul_kernel(*refs):
    x_ref, a_refs, w_ref, o_ref = refs[0], refs[1:-2], refs[-2], refs[-1]
    a = [r[...] for r in a_refs]
    a = a[0] if len(a) == 1 else jnp.concatenate(a, axis=1)
    o_ref[...] = x_ref[...] + _dot(a, w_ref[...])


def _resid_matmul(x, acts, w, tm):
    n, d = x.shape
    return pl.pallas_call(
        _resid_matmul_kernel,
        out_shape=jax.ShapeDtypeStruct((n, d), F32),
        grid=(n // tm,),
        in_specs=[pl.BlockSpec((tm, d), lambda i: (i, 0))]
        + [pl.BlockSpec((tm, a.shape[1]), lambda i: (i, 0)) for a in acts]
        + [pl.BlockSpec(w.shape, lambda i: (0, 0))],
        out_specs=pl.BlockSpec((tm, d), lambda i: (i, 0)),
        compiler_params=_params("parallel"),
        name="resid_matmul",
    )(x, *acts, w)


def _cumsum_rows(g):
    rows = lax.broadcasted_iota(jnp.int32, g.shape, 0)
    b, sh = g, 1
    while sh < g.shape[0]:
        b = b + jnp.where(rows >= sh, pltpu.roll(b, sh, 0), 0.0)
        sh *= 2
    return b


def _hgrn_chunk(zq, zf, v, lb, st):
    c = zq.shape[0]
    f = lb + (1.0 - lb) * jax.nn.sigmoid(zf)
    g = jnp.log(jnp.maximum(f, TINY))
    kk = 1.0 - f
    q = zq * jax.nn.sigmoid(zq)
    b = _cumsum_rows(g)

    o = _dot_nt((q * jnp.exp(b)).astype(BF16), st.astype(BF16))
    b_last = b[c - 1:c]
    kdec = (kk * jnp.exp(b_last - b)).astype(BF16)
    st_new = st * jnp.exp(b_last) + _dot(v.T.astype(BF16), kdec)

    rows = lax.broadcasted_iota(jnp.int32, (HGRN_SUB, 1), 0)
    sl = TOKEN_SUBLANES
    a_rows, d_rows = [], []
    for i in range(c // HGRN_SUB):
        r0 = i * HGRN_SUB
        bi, qi, ki, vi = (t[r0:r0 + HGRN_SUB] for t in (b, q, kk, v))
        if i == 0:
            a_rows.append(jnp.zeros((HGRN_SUB, c), F32))
        else:
            beta = b[r0 - 1:r0]
            qs = (qi * jnp.exp(bi - beta)).astype(BF16)
            ks = kk[:r0] * jnp.exp(beta - b[:r0])
            ks = jnp.concatenate([ks, jnp.zeros((c - r0, ks.shape[1]), F32)], axis=0).astype(BF16)
            a_rows.append(_dot_nt(qs, ks))
        diag = [jnp.zeros((sl, v.shape[1]), F32) for _ in range(HGRN_SUB // sl)]
        for s in range(HGRN_SUB):
            lo = s // sl * sl
            w = qi[lo:] * jnp.exp(jnp.minimum(bi[lo:] - bi[s:s + 1], 0.0)) * ki[s:s + 1]
            rs = jnp.where(rows[lo:] >= s, jnp.sum(w, axis=1, keepdims=True), 0.0)
            contrib = rs * vi[s:s + 1]
            for g in range(lo // sl, HGRN_SUB // sl):
                diag[g] = diag[g] + contrib[g * sl - lo:(g + 1) * sl - lo]
        d_rows += diag
    a = jnp.concatenate(a_rows, axis=0).astype(BF16)
    o = o + _dot(a, v.astype(BF16)) + jnp.concatenate(d_rows, axis=0)
    return o, st_new


def _hgrn_kernel(zq_ref, zf_ref, zi_ref, zg_ref, lbp_ref, gn_ref, o_ref, st_ref, *, layer):
    @pl.when(pl.program_id(2) == 0)
    def _():
        st_ref[...] = jnp.zeros_like(st_ref)

    lbp = lbp_ref[...]
    e = jnp.exp(lbp - jnp.max(lbp, axis=0, keepdims=True))
    p = e / jnp.sum(e, axis=0, keepdims=True)
    lb = jnp.zeros((1, lbp.shape[1]), F32)
    for r in range(1, layer + 1):
        lb = lb + p[r:r + 1]
    gn = gn_ref[...]

    def chunk(ci, carry):
        r0 = pl.multiple_of(ci * HGRN_CHUNK, HGRN_CHUNK)
        sl = pl.ds(r0, HGRN_CHUNK)
        for hd in range(HGRN_HEADS_PER_STEP):
            cols = slice(hd * HGRN_DK, (hd + 1) * HGRN_DK)
            o, st_new = _hgrn_chunk(zq_ref[sl, cols].astype(F32), zf_ref[sl, cols].astype(F32),
                                    zi_ref[sl, cols].astype(F32), lb[:, cols], st_ref[hd])
            st_ref[hd] = st_new
            o = _rms(o, gn) * jax.nn.sigmoid(zg_ref[sl, cols].astype(F32))
            o_ref[sl, cols] = o.astype(o_ref.dtype)
        return carry

    lax.fori_loop(0, o_ref.shape[0] // HGRN_CHUNK, chunk, 0)


def _hgrn(proj, hgrn_lb, gn, layer, batch, seq):
    n = proj.shape[0]
    ns = seq // HGRN_ROWS
    hps = HGRN_HEADS_PER_STEP
    hg = HGRN_HEADS // hps
    width = hps * HGRN_DK

    def col(k):
        return pl.BlockSpec((HGRN_ROWS, width), lambda b, h, s: (b * ns + s, k * hg + h))

    return pl.pallas_call(
        functools.partial(_hgrn_kernel, layer=layer),
        out_shape=jax.ShapeDtypeStruct((n, HGRN_WIDTH), BF16),
        grid=(batch, hg, ns),
        in_specs=[col(0), col(1), col(2), col(3),
                  pl.BlockSpec((hgrn_lb.shape[0], width), lambda b, h, s: (0, h)),
                  pl.BlockSpec((1, HGRN_DK), lambda b, h, s: (0, 0))],
        out_specs=pl.BlockSpec((HGRN_ROWS, width), lambda b, h, s: (b * ns + s, h)),
        scratch_shapes=[pltpu.VMEM((hps, HGRN_DK, HGRN_DK), F32)],
        compiler_params=_params("parallel", "parallel", "arbitrary"),
        name="hgrn2",
    )(proj, proj, proj, proj, hgrn_lb, gn.reshape(1, HGRN_DK))


def _conv_kernel(cu_ref, cv_ref, dw_ref, db_ref, g_ref, b_ref, o_ref, ext_ref, rot_ref):
    rows = o_ref.shape[0]

    @pl.when(pl.program_id(1) == 0)
    def _():
        ext_ref[0:CONV_HALO, :] = jnp.zeros((CONV_HALO, ext_ref.shape[1]), F32)

    @pl.when(pl.program_id(1) > 0)
    def _():
        ext_ref[0:CONV_HALO, :] = ext_ref[rows:rows + CONV_HALO, :]

    ext_ref[CONV_HALO:, :] = cu_ref[...].astype(F32) * jax.nn.sigmoid(cv_ref[...].astype(F32))
    span = rot_ref.shape[1]
    for r in range(1, TOKEN_SUBLANES):
        rot_ref[r - 1] = ext_ref[r:r + span, :]
    acc = jnp.zeros(o_ref.shape, F32) + db_ref[...]
    first = CONV_HALO - (CONV_WIDTH - 1)
    for j in range(CONV_WIDTH):
        a, r = divmod(first + j, TOKEN_SUBLANES)
        a *= TOKEN_SUBLANES
        tap = ext_ref[a:a + rows, :] if r == 0 else rot_ref[r - 1, a:a + rows, :]
        acc = acc + dw_ref[j:j + 1, :] * tap
    mu = jnp.mean(acc, axis=-1, keepdims=True)
    cen = acc - mu
    var = jnp.mean(cen * cen, axis=-1, keepdims=True)
    y = cen * lax.rsqrt(var + EPS) * g_ref[...] + b_ref[...]
    o_ref[...] = (y * jax.nn.sigmoid(y)).astype(o_ref.dtype)


def _conv(proj, dw, db, ln_g, ln_b, batch, seq):
    n = proj.shape[0]
    ch = dw.shape[1]
    ns = seq // CONV_ROWS
    first_blk = 4 * HGRN_WIDTH // ch
    vec = pl.BlockSpec((1, ch), lambda b, s: (0, 0))
    return pl.pallas_call(
        _conv_kernel,
        out_shape=jax.ShapeDtypeStruct((n, ch), BF16),
        grid=(batch, ns),
        in_specs=[pl.BlockSpec((CONV_ROWS, ch), lambda b, s: (b * ns + s, first_blk)),
                  pl.BlockSpec((CONV_ROWS, ch), lambda b, s: (b * ns + s, first_blk + 1)),
                  pl.BlockSpec((CONV_WIDTH, ch), lambda b, s: (0, 0)), vec, vec, vec],
        out_specs=pl.BlockSpec((CONV_ROWS, ch), lambda b, s: (b * ns + s, 0)),
        scratch_shapes=[pltpu.VMEM((CONV_ROWS + CONV_HALO, ch), F32),
                        pltpu.VMEM((TOKEN_SUBLANES - 1, CONV_ROWS + CONV_HALO - TOKEN_SUBLANES, ch), F32)],
        compiler_params=_params("parallel", "arbitrary"),
        name="glu_conv_ln",
    )(proj, proj, dw, db.reshape(1, ch), ln_g.reshape(1, ch), ln_b.reshape(1, ch))


def _bias_kernel(tab_ref, o_ref):
    h = pl.program_id(0)
    t = o_ref.shape[2]
    k = lax.broadcasted_iota(jnp.int32, (t, t), 0)
    q = lax.broadcasted_iota(jnp.int32, (t, t), 1)
    max_exact = REL_BUCKETS // 2
    for d in range(2):
        rel = q - k + d * t
        nn = jnp.maximum(rel, 0)
        nf = jnp.maximum(nn, 1).astype(F32)
        large = max_exact + (jnp.log(nf / max_exact) / math.log(REL_MAX_DIST / max_exact)
                             * (REL_BUCKETS - max_exact)).astype(jnp.int32)
        large = jnp.clip(large, 0, REL_BUCKETS - 1)
        bucket = jnp.where(nn < max_exact, nn, large)
        val = jnp.zeros((t, t), F32)
        for bk in range(REL_BUCKETS):
            val = jnp.where(bucket == bk, tab_ref[bk, h], val)
        val = (val - tab_ref[REL_BUCKETS - 1, h]) * LOG2E
        if d == 0:
            val = jnp.where(rel >= 0, val, NEG_BIG)
        o_ref[0, d] = val
    o_ref[0, 2] = jnp.zeros((t, t), F32)


def _bias_tiles(rel_bias):
    assert ATT_TILE >= REL_MAX_DIST
    heads = rel_bias.shape[1]
    return pl.pallas_call(
        _bias_kernel,
        out_shape=jax.ShapeDtypeStruct((heads, 3, ATT_TILE, ATT_TILE), F32),
        grid=(heads,),
        in_specs=[pl.BlockSpec(memory_space=pltpu.SMEM)],
        out_specs=pl.BlockSpec((1, 3, ATT_TILE, ATT_TILE), lambda h: (h, 0, 0, 0)),
        compiler_params=_params("parallel"),
        name="t5_bias_tiles",
    )(rel_bias)


def _attn_kernel(q_ref, k_ref, v_ref, bias_ref, lamv_ref, g_ref, o_ref,
                 vt_ref, qs_ref, m_ref, l_ref, acc_ref, sa_ref, sb_ref, cma_ref, cmb_ref, *, lam_init):
    i = pl.program_id(2)
    t = o_ref.shape[0]

    @pl.when(i == 0)
    def _():
        for j in range(vt_ref.shape[0]):
            vt_ref[j] = v_ref[j * t:(j + 1) * t, :].astype(F32).T.astype(BF16)

    lane = lax.broadcasted_iota(jnp.int32, (1, DA_VDIM), 1)
    q = q_ref[...].astype(F32) * (DA_HEAD_DIM ** -0.5 * LOG2E)
    qs_ref[0] = jnp.where(lane < DA_HEAD_DIM, q, 0.0).astype(BF16)
    qs_ref[1] = jnp.where(lane >= DA_HEAD_DIM, q, 0.0).astype(BF16)
    m_ref[...] = jnp.full(m_ref.shape, -jnp.inf, F32)
    l_ref[...] = jnp.zeros(l_ref.shape, F32)
    acc_ref[...] = jnp.zeros(acc_ref.shape, F32)

    def produce(j, s_dst, cm_dst, far=False):
        kb = k_ref[pl.ds(pl.multiple_of(j * t, t), t), :]
        bt = None if far else bias_ref[0, jnp.minimum(i - j, 2)]
        for mp in range(2):
            s = _dot_nt(kb, qs_ref[mp])
            s = s if far else s + bt
            s_dst[mp] = s
            cm_dst[mp] = jnp.max(s, axis=0, keepdims=True)

    def consume(j, s_src, cm_src):
        vt = vt_ref[j]
        for mp in range(2):
            for st in range(t // ATT_STRIP):
                cols = slice(st * ATT_STRIP, (st + 1) * ATT_STRIP)
                m_old = m_ref[mp, :, cols]
                mn = jnp.maximum(m_old, cm_src[mp, :, cols])
                p = jnp.exp2(s_src[mp, :, cols] - mn)
                al = jnp.exp2(m_old - mn)
                m_ref[mp, :, cols] = mn
                l_ref[mp, :, cols] = l_ref[mp, :, cols] * al + jnp.sum(p, axis=0, keepdims=True)
                acc_ref[mp, :, cols] = acc_ref[mp, :, cols] * al + _dot(vt, p.astype(BF16))

    produce(0, sa_ref, cma_ref)

    def pair(pi, carry, far):
        j = 2 * pi
        produce(j + 1, sb_ref, cmb_ref, far)
        consume(j, sa_ref, cma_ref)
        produce(j + 2, sa_ref, cma_ref, far)
        consume(j + 1, sb_ref, cmb_ref)
        return carry

    n_far = jnp.maximum(i - 2, 0) >> 1
    lax.fori_loop(0, n_far, functools.partial(pair, far=True), 0)
    lax.fori_loop(n_far, i >> 1, functools.partial(pair, far=False), 0)

    @pl.when((i & 1) == 0)
    def _():
        consume(i, sa_ref, cma_ref)

    @pl.when((i & 1) == 1)
    def _():
        produce(i, sb_ref, cmb_ref)
        consume(i - 1, sa_ref, cma_ref)
        consume(i, sb_ref, cmb_ref)

    lv = lamv_ref[...]
    lam = (jnp.exp(jnp.sum(lv[0:1] * lv[1:2], axis=1, keepdims=True))
           - jnp.exp(jnp.sum(lv[2:3] * lv[3:4], axis=1, keepdims=True)) + lam_init)
    o = acc_ref[0] * (1.0 / l_ref[0]) - lam * (acc_ref[1] * (1.0 / l_ref[1]))
    y = o * lax.rsqrt(jnp.mean(o * o, axis=0, keepdims=True) + EPS) * g_ref[...] * (1.0 - lam_init)
    o_ref[...] = y.T.astype(o_ref.dtype)


def _diff_attn(qkv, bias_tiles, lamv, subln_g, lam_init, batch, seq):
    n = qkv.shape[0]
    t = ATT_TILE
    nq = seq // t
    hh = DA_HEADS
    return pl.pallas_call(
        functools.partial(_attn_kernel, lam_init=lam_init),
        out_shape=jax.ShapeDtypeStruct((n, hh * DA_VDIM), BF16),
        grid=(batch, hh, nq),
        in_specs=[pl.BlockSpec((t, DA_VDIM), lambda b, h, i: (b * nq + i, h)),
                  pl.BlockSpec((seq, DA_VDIM), lambda b, h, i: (b, hh + h)),
                  pl.BlockSpec((seq, DA_VDIM), lambda b, h, i: (b, 2 * hh + h)),
                  pl.BlockSpec((1, 3, t, t), lambda b, h, i: (h, 0, 0, 0)),
                  pl.BlockSpec(lamv.shape, lambda b, h, i: (0, 0)),
                  pl.BlockSpec((DA_VDIM, 1), lambda b, h, i: (0, 0))],
        out_specs=pl.BlockSpec((t, DA_VDIM), lambda b, h, i: (b * nq + i, h)),
        scratch_shapes=[pltpu.VMEM((nq, DA_VDIM, t), BF16), pltpu.VMEM((2, t, DA_VDIM), BF16),
                        pltpu.VMEM((2, 1, t), F32), pltpu.VMEM((2, 1, t), F32),
                        pltpu.VMEM((2, DA_VDIM, t), F32),
                        pltpu.VMEM((2, t, t), F32), pltpu.VMEM((2, t, t), F32),
                        pltpu.VMEM((2, 1, t), F32), pltpu.VMEM((2, 1, t), F32)],
        compiler_params=_params("parallel", "parallel", "arbitrary"),
        name="diff_attention",
    )(qkv, qkv, qkv, bias_tiles, lamv, subln_g.reshape(DA_VDIM, 1))


def _xattn_kernel(x_ref, g_ref, wq_ref, kv_ref, wo_ref, o_ref):
    x = x_ref[...]
    d = x.shape[1]
    dh = d // XA_HEADS
    h = _rms(x, g_ref[...]).astype(BF16)
    q = (_dot(h, wq_ref[...]) * (dh ** -0.5)).astype(BF16)
    kv = kv_ref[...]
    outs = []
    for hd in range(XA_HEADS):
        s = _dot_nt(q[:, hd * dh:(hd + 1) * dh], kv[:, hd * dh:(hd + 1) * dh])
        p = jnp.exp(s - jnp.max(s, axis=-1, keepdims=True))
        pv = _dot(p.astype(BF16), kv[:, d + hd * dh:d + (hd + 1) * dh])
        outs.append(pv * (1.0 / jnp.sum(p, axis=-1, keepdims=True)))
    o = jnp.concatenate(outs, axis=1).astype(BF16)
    o_ref[...] = x + _dot(o, wo_ref[...])


def _xattn(x, g, wq, kv, wo, seq, mem_len, tm):
    n, d = x.shape
    per_batch = seq // tm
    return pl.pallas_call(
        _xattn_kernel,
        out_shape=jax.ShapeDtypeStruct((n, d), F32),
        grid=(n // tm,),
        in_specs=[pl.BlockSpec((tm, d), lambda i: (i, 0)),
                  pl.BlockSpec((1, d), lambda i: (0, 0)),
                  pl.BlockSpec((d, d), lambda i: (0, 0)),
                  pl.BlockSpec((mem_len, 2 * d), lambda i: (i // per_batch, 0)),
                  pl.BlockSpec((d, d), lambda i: (0, 0))],
        out_specs=pl.BlockSpec((tm, d), lambda i: (i, 0)),
        compiler_params=_params("parallel"),
        name="memory_xattn",
    )(x, g.reshape(1, d), wq, kv, wo)


def _route_t(lg):
    row = lax.broadcasted_iota(jnp.int32, lg.shape, 0)
    big = lg.shape[0]
    is_grp = row < N_GROUPS
    gl = jnp.where(is_grp, lg, -jnp.inf)
    gmax = jnp.max(gl, axis=0, keepdims=True)
    gsel = jnp.min(jnp.where(gl == gmax, row, big), axis=0, keepdims=True)
    g_w = 1.0 / jnp.sum(jnp.where(is_grp, jnp.exp(lg - gmax), 0.0), axis=0, keepdims=True)
    e_idx = row - N_GROUPS
    in_grp = (e_idx >= 0) & (e_idx < N_EXPERTS) & (e_idx // EXPERTS_PER_GROUP == gsel)
    ml = jnp.where(in_grp, lg, -jnp.inf)
    v1 = jnp.max(ml, axis=0, keepdims=True)
    i1 = jnp.min(jnp.where(ml == v1, row, big), axis=0, keepdims=True)
    ml2 = jnp.where(row == i1, -jnp.inf, ml)
    v2 = jnp.max(ml2, axis=0, keepdims=True)
    i2 = jnp.min(jnp.where(ml2 == v2, row, big), axis=0, keepdims=True)
    t = jnp.exp(v2 - v1)
    w1 = g_w / (1.0 + t)
    w2 = g_w * t / (1.0 + t)
    return gsel, jnp.where(row == i1, w1, 0.0) + jnp.where(row == i2, w2, 0.0)


def _router_kernel(x_ref, g_ref, wrt_ref, brt_ref, tok_ref, idx_ref, cnt_ref, tri_ref, run_ref):
    t, d = x_ref.shape

    @pl.when(pl.program_id(0) == 0)
    def _():
        r = lax.broadcasted_iota(jnp.int32, (t, t), 0)
        c = lax.broadcasted_iota(jnp.int32, (t, t), 1)
        tri_ref[...] = (r < c).astype(BF16)
        run_ref[...] = jnp.zeros_like(run_ref)

    h = _rms(x_ref[...], g_ref[...])
    logits = lax.dot_general(wrt_ref[...], h, (((1,), (1,)), ((), ())), preferred_element_type=F32,
                             precision=lax.Precision.HIGHEST) + brt_ref[...]
    gsel, gate = _route_t(logits[0:ROUTE_ROWS])

    row8 = lax.broadcasted_iota(jnp.int32, (8, t), 0)
    memb = (row8 == gsel).astype(F32)
    before = _dot(memb.astype(BF16), tri_ref[...]) + run_ref[...]
    pos = jnp.sum(memb * before, axis=0, keepdims=True)
    run_ref[...] = run_ref[...] + jnp.sum(memb, axis=1, keepdims=True)
    cnt_ref[...] = run_ref[:, 0:LANES]
    idx_ref[0] = jnp.concatenate([gsel.astype(F32), pos, jnp.zeros((6, t), F32)], axis=0)

    bits = lax.bitcast_convert_type(h.astype(BF16).astype(F32), jnp.uint32)
    words = (bits[:, 0:d // 2] >> 16) | bits[:, d // 2:]
    h_sublanes = d // 2 // LANES
    for s in range(h_sublanes):
        tok_ref[_token_rows(s, t)] = words[:, s * LANES:(s + 1) * LANES]
    g16 = gate[N_GROUPS:N_GROUPS + N_EXPERTS]
    gate_rows = jnp.concatenate([g16, jnp.zeros((LANES - N_EXPERTS, t), F32)], axis=0).T
    tok_ref[_token_rows(h_sublanes, t)] = lax.bitcast_convert_type(gate_rows, jnp.uint32)
    for s in range(h_sublanes + 1, TOKEN_SUBLANES):
        tok_ref[_token_rows(s, t)] = jnp.zeros((t, LANES), jnp.uint32)


def _router(x, g, wrt, brt, tm):
    n, d = x.shape
    nt = n // tm
    assert d // 2 // LANES + 1 <= TOKEN_SUBLANES
    return pl.pallas_call(
        _router_kernel,
        out_shape=(jax.ShapeDtypeStruct((n * TOKEN_SUBLANES, LANES), jnp.uint32),
                   jax.ShapeDtypeStruct((nt, 8, tm), F32),
                   jax.ShapeDtypeStruct((8, LANES), F32)),
        grid=(nt,),
        in_specs=[pl.BlockSpec((tm, d), lambda i: (i, 0)),
                  pl.BlockSpec((1, d), lambda i: (0, 0)),
                  pl.BlockSpec((LANES, d), lambda i: (0, 0)),
                  pl.BlockSpec((LANES, 1), lambda i: (0, 0))],
        out_specs=(pl.BlockSpec((tm * TOKEN_SUBLANES, LANES), lambda i: (i, 0)),
                   pl.BlockSpec((1, 8, tm), lambda i: (i, 0, 0)),
                   pl.BlockSpec((8, LANES), lambda i: (0, 0))),
        scratch_shapes=[pltpu.VMEM((tm, tm), BF16), pltpu.VMEM((8, tm), F32)],
        compiler_params=_params("arbitrary"),
        name="moe_router",
    )(x, g.reshape(1, d), wrt, brt)


def _permute_kernel(idx_ref, src_ref, *rest, scatter):
    dst_ref, sem = rest[-2], rest[-1]
    rows = idx_ref.shape[2]

    def copy(r):
        j = idx_ref[0, 0, r]
        src_row, dst_row = (r, j) if scatter else (j, r)
        return pltpu.make_async_copy(src_ref.at[pl.ds(src_row, 1)], dst_ref.at[pl.ds(dst_row, 1)], sem)

    def start(k, carry):
        for u in range(PERMUTE_UNROLL):
            copy(k * PERMUTE_UNROLL + u).start(priority=u % 2)
        return carry

    def wait(r, carry):
        copy(r).wait()
        return carry

    lax.fori_loop(0, rows // PERMUTE_UNROLL, start, 0)
    lax.fori_loop(0, rows, wait, 0, unroll=PERMUTE_UNROLL)


def _permute_tokens(idx, src, dst_tokens, scatter, init=None):
    n = idx.shape[0]
    rows = PERMUTE_ROWS
    tile = (TOKEN_SUBLANES, LANES)
    any_spec = pl.BlockSpec(memory_space=pl.ANY)
    block = pl.BlockSpec((rows, *tile), lambda i: (i, 0, 0))
    ins = [idx.reshape(n // rows, 1, rows), src.reshape(-1, *tile)]
    if init is not None:
        ins.append(init.reshape(-1, *tile))
    out = pl.pallas_call(
        functools.partial(_permute_kernel, scatter=scatter),
        out_shape=jax.ShapeDtypeStruct((dst_tokens, *tile), src.dtype),
        grid=(n // rows,),
        in_specs=[pl.BlockSpec((1, 1, rows), lambda i: (i, 0, 0), memory_space=pltpu.SMEM),
                  block if scatter else any_spec] + ([any_spec] if init is not None else []),
        out_specs=any_spec if scatter else block,
        scratch_shapes=[pltpu.SemaphoreType.DMA],
        input_output_aliases={2: 0} if init is not None else {},
        compiler_params=_params("arbitrary"),
        name="moe_scatter_tokens" if scatter else "moe_gather_tokens",
    )(*ins)
    return out.reshape(dst_tokens * TOKEN_SUBLANES, LANES)


def _experts_kernel(grp_ref, nblk_ref, tok_ref, w1_ref, w3_ref, w2_ref, o_ref):
    i = pl.program_id(0)
    tb = o_ref.shape[0] // TOKEN_SUBLANES
    d = w1_ref.shape[2]
    h_sublanes = d // 2 // LANES

    @pl.when(i < nblk_ref[0])
    def _():
        words = [tok_ref[_token_rows(s, tb)] for s in range(h_sublanes)]
        low = [lax.bitcast_convert_type(w << 16, F32).astype(BF16) for w in words]
        high = [lax.bitcast_convert_type(w & jnp.uint32(0xFFFF0000), F32).astype(BF16) for w in words]
        h = jnp.concatenate(low + high, axis=1)
        gates = lax.bitcast_convert_type(tok_ref[_token_rows(h_sublanes, tb)], F32)
        lane = lax.broadcasted_iota(jnp.int32, (1, LANES), 1)
        first = grp_ref[i] * EXPERTS_PER_GROUP
        acc = jnp.zeros((tb, d), F32)
        for e in range(EXPERTS_PER_GROUP):
            gate = jnp.sum(jnp.where(lane == first + e, gates, 0.0), axis=1, keepdims=True)
            h1 = _dot(h, w1_ref[0, e])
            hid = h1 * jax.nn.sigmoid(h1) * _dot(h, w3_ref[0, e]) * gate
            acc = acc + _dot(hid.astype(BF16), w2_ref[0, e])
        for s in range(TOKEN_SUBLANES):
            o_ref[_token_rows(s, tb)] = acc[:, s * LANES:(s + 1) * LANES]

    @pl.when(i >= nblk_ref[0])
    def _():
        o_ref[...] = jnp.zeros_like(o_ref)


def _experts(toks, blk_grp, nblk, w1, w3, w2):
    d, ff = w1.shape[-2:]
    assert d == TOKEN_SUBLANES * LANES
    tb = MOE_BLOCK
    epg = EXPERTS_PER_GROUP
    tile_rows = pl.BlockSpec((tb * TOKEN_SUBLANES, LANES), lambda i, grp, nb: (i, 0))
    grid_spec = pltpu.PrefetchScalarGridSpec(
        num_scalar_prefetch=2,
        grid=(toks.shape[0] // (tb * TOKEN_SUBLANES),),
        in_specs=[tile_rows,
                  pl.BlockSpec((1, epg, d, ff), lambda i, grp, nb: (grp[i], 0, 0, 0)),
                  pl.BlockSpec((1, epg, d, ff), lambda i, grp, nb: (grp[i], 0, 0, 0)),
                  pl.BlockSpec((1, epg, ff, d), lambda i, grp, nb: (grp[i], 0, 0, 0))],
        out_specs=tile_rows,
    )
    return pl.pallas_call(
        _experts_kernel,
        out_shape=jax.ShapeDtypeStruct(toks.shape, F32),
        grid_spec=grid_spec,
        compiler_params=_params("arbitrary"),
        name="moe_experts",
    )(blk_grp, nblk, toks, w1.reshape(N_GROUPS, epg, d, ff), w3.reshape(N_GROUPS, epg, d, ff),
      w2.reshape(N_GROUPS, epg, ff, d))


def _moe(x, g, w_grp, b_grp, w_exp, b_exp, w1, w3, w2):
    n, d = x.shape
    pad = LANES - N_GROUPS - N_EXPERTS
    wrt = jnp.concatenate([w_grp, w_exp, jnp.zeros((d, pad), F32)], axis=1).T
    brt = jnp.concatenate([b_grp, b_exp, jnp.zeros((pad,), F32)]).reshape(LANES, 1)
    toks, idx, cnt = _router(x, g, wrt, brt, MOE_ROWS)

    tb = MOE_BLOCK
    counts = cnt[0:N_GROUPS, 0].astype(jnp.int32)
    padded = (counts + tb - 1) // tb * tb
    ends = jnp.cumsum(padded)
    offs = ends - padded
    gsel = idx[:, 0, :].reshape(n).astype(jnp.int32)
    group_ids = jnp.arange(N_GROUPS, dtype=jnp.int32)
    dest = (jnp.sum(jnp.where(gsel[:, None] == group_ids[None, :], offs[None, :], 0), axis=1)
            + idx[:, 1, :].reshape(n).astype(jnp.int32))
    nblk_max = n // tb + N_GROUPS
    blk_start = jnp.arange(nblk_max, dtype=jnp.int32) * tb
    blk_grp = jnp.minimum(jnp.sum(blk_start[:, None] >= ends[None, :], axis=1), N_GROUPS - 1).astype(jnp.int32)
    nblk = (ends[N_GROUPS - 1:] // tb).astype(jnp.int32)

    sorted_toks = _permute_tokens(dest, toks, nblk_max * tb, True,
                                  init=jnp.zeros((nblk_max * tb * TOKEN_SUBLANES, LANES), jnp.uint32))
    ys = _experts(sorted_toks, blk_grp, nblk, w1, w3, w2)
    return _permute_tokens(dest, ys, n, False)


def _final_norm_kernel(x_ref, y_ref, g_ref, o_ref):
    o_ref[...] = _rms(x_ref[...] + _from_token_tiles(y_ref), g_ref[...])


def _final_norm(x, y, g, tm):
    n, d = x.shape
    row = pl.BlockSpec((tm, d), lambda i: (i, 0))
    return pl.pallas_call(
        _final_norm_kernel,
        out_shape=jax.ShapeDtypeStruct((n, d), F32),
        grid=(n // tm,),
        in_specs=[row, pl.BlockSpec((tm * TOKEN_SUBLANES, LANES), lambda i: (i, 0)),
                  pl.BlockSpec((1, d), lambda i: (0, 0))],
        out_specs=row,
        compiler_params=_params("parallel"),
        name="final_norm",
    )(x, y, g.reshape(1, d))


def kernel(x, mem, rel_bias, mem_norm_g, final_norm_g, hgrn_lb, norm_mix_g, norm_xattn_g, norm_ffn_g, even_w_in, even_w_out, hgrn_norm_g, conv_dw, conv_db, conv_ln_g, conv_ln_b, odd_w_qkv, odd_w_out, lam_q1, lam_k1, lam_q2, lam_k2, subln_g, xa_wq, xa_wkv, xa_wo, moe_w_grp, moe_b_grp, moe_w_exp, moe_b_exp, moe_w1, moe_w3, moe_w2):
    batch, seq, d = x.shape
    mem_len = mem.shape[1]
    depth = norm_mix_g.shape[0]
    n = batch * seq
    assert seq % ATT_TILE == 0 and seq % HGRN_ROWS == 0 and seq % CONV_ROWS == 0 and seq % ROW_TILE == 0
    assert n % MOE_ROWS == 0 and n % MOE_BLOCK == 0 and n % PERMUTE_ROWS == 0
    assert (batch * mem_len) % ROW_TILE == 0

    xf = x.reshape(n, d)
    memf = mem.reshape(batch * mem_len, d)
    bias_tiles = _bias_tiles(rel_bias)
    y_moe = None

    for i in range(depth):
        j = i // 2
        w_in = (even_w_in if i % 2 == 0 else odd_w_qkv)[j].astype(BF16)
        if y_moe is None:
            proj = _norm_matmul(xf, norm_mix_g[i], w_in, ROW_TILE)
        else:
            xf, proj = _add_norm_matmul(xf, y_moe, norm_mix_g[i], w_in, ROW_TILE)
        if i % 2 == 0:
            o_hgrn = _hgrn(proj, hgrn_lb, hgrn_norm_g[j], j, batch, seq)
            o_conv = _conv(proj, conv_dw[j], conv_db[j], conv_ln_g[j], conv_ln_b[j], batch, seq)
            xf = _resid_matmul(xf, [o_hgrn, o_conv], even_w_out[j].astype(BF16), ROW_TILE)
        else:
            lam_init = 0.8 - 0.6 * math.exp(-0.3 * i)
            lamv = jnp.stack([lam_q1[j], lam_k1[j], lam_q2[j], lam_k2[j]]).astype(F32)
            o_att = _diff_attn(proj, bias_tiles, lamv, subln_g[j], lam_init, batch, seq)
            xf = _resid_matmul(xf, [o_att], odd_w_out[j].astype(BF16), ROW_TILE)

        kv = _norm_matmul(memf, mem_norm_g, xa_wkv[i].astype(BF16), ROW_TILE)
        xf = _xattn(xf, norm_xattn_g[i], xa_wq[i].astype(BF16), kv, xa_wo[i].astype(BF16), seq, mem_len, ROW_TILE)

        y_moe = _moe(xf, norm_ffn_g[i], moe_w_grp[i], moe_b_grp[i], moe_w_exp[i], moe_b_exp[i],
                     moe_w1[i].astype(BF16), moe_w3[i].astype(BF16), moe_w2[i].astype(BF16))

    return _final_norm(xf, y_moe, final_norm_g, ROW_TILE).reshape(batch, seq, d)
```

```python
import functools
import math

import jax
import jax.numpy as jnp
from jax import lax
from jax.experimental import pallas as pl
from jax.experimental.pallas import tpu as pltpu

F32 = jnp.float32
BF16 = jnp.bfloat16

EPS = 1e-6
NEG_BIG = -1e30
LOG2E = math.log2(math.e)
TINY = 1e-30

HGRN_HEADS = 4
HGRN_DK = 128
HGRN_WIDTH = HGRN_HEADS * HGRN_DK
CONV_WIDTH = 31
DA_HEADS = 8
DA_HEAD_DIM = 64
DA_VDIM = 2 * DA_HEAD_DIM
REL_BUCKETS = 32
REL_MAX_DIST = 128
XA_HEADS = 4
N_GROUPS = 4
EXPERTS_PER_GROUP = 4
N_EXPERTS = N_GROUPS * EXPERTS_PER_GROUP

LANES = 128
TOKEN_SUBLANES = 8
V7X_VMEM_LIMIT_BYTES = 56 * 1024 * 1024

ROW_TILE = 512
HGRN_CHUNK = 128
HGRN_SUB = 16
HGRN_HEADS_PER_STEP = 4
HGRN_ROWS = 512
CONV_ROWS = 512
CONV_HALO = 32
ATT_TILE = 512
ATT_STRIP = 256
MOE_ROWS = 1024
ROUTE_ROWS = 24
MOE_BLOCK = 1024
PERMUTE_ROWS = 1024
PERMUTE_UNROLL = 8


def _params(*sem):
    return pltpu.CompilerParams(dimension_semantics=sem, vmem_limit_bytes=V7X_VMEM_LIMIT_BYTES)


def _rms(x, g):
    return x * lax.rsqrt(jnp.mean(x * x, axis=-1, keepdims=True) + EPS) * g


def _dot(a, b):
    return jnp.dot(a, b, preferred_element_type=F32)


def _dot_nt(a, b):
    return lax.dot_general(a, b, (((1,), (1,)), ((), ())), preferred_element_type=F32)


def _token_rows(s, n):
    return (pl.ds(s, n, stride=TOKEN_SUBLANES), slice(None))


def _from_token_tiles(y_ref):
    n = y_ref.shape[0] // TOKEN_SUBLANES
    return jnp.concatenate([y_ref[_token_rows(s, n)] for s in range(TOKEN_SUBLANES)], axis=1)


def _norm_matmul_kernel(x_ref, g_ref, w_ref, o_ref):
    h = _rms(x_ref[...], g_ref[...]).astype(BF16)
    o_ref[...] = _dot(h, w_ref[...]).astype(o_ref.dtype)


def _norm_matmul(x, g, w, tm):
    n, d = x.shape
    nout = w.shape[1]
    return pl.pallas_call(
        _norm_matmul_kernel,
        out_shape=jax.ShapeDtypeStruct((n, nout), BF16),
        grid=(n // tm,),
        in_specs=[pl.BlockSpec((tm, d), lambda i: (i, 0)),
                  pl.BlockSpec((1, d), lambda i: (0, 0)),
                  pl.BlockSpec((d, nout), lambda i: (0, 0))],
        out_specs=pl.BlockSpec((tm, nout), lambda i: (i, 0)),
        compiler_params=_params("parallel"),
        name="norm_matmul",
    )(x, g.reshape(1, d), w)


def _add_norm_matmul_kernel(x_ref, y_ref, g_ref, w_ref, xo_ref, o_ref):
    x = x_ref[...] + _from_token_tiles(y_ref)
    xo_ref[...] = x
    o_ref[...] = _dot(_rms(x, g_ref[...]).astype(BF16), w_ref[...]).astype(o_ref.dtype)


def _add_norm_matmul(x, y, g, w, tm):
    n, d = x.shape
    nout = w.shape[1]
    row = pl.BlockSpec((tm, d), lambda i: (i, 0))
    tiles = pl.BlockSpec((tm * TOKEN_SUBLANES, LANES), lambda i: (i, 0))
    return pl.pallas_call(
        _add_norm_matmul_kernel,
        out_shape=(jax.ShapeDtypeStruct((n, d), F32), jax.ShapeDtypeStruct((n, nout), BF16)),
        grid=(n // tm,),
        in_specs=[row, tiles, pl.BlockSpec((1, d), lambda i: (0, 0)), pl.BlockSpec((d, nout), lambda i: (0, 0))],
        out_specs=(row, pl.BlockSpec((tm, nout), lambda i: (i, 0))),
        compiler_params=_params("parallel"),
        name="add_norm_matmul",
    )(x, y, g.reshape(1, d), w)


def _resid_matmul_kernel(*refs):
    x_ref, a_refs, w_ref, o_ref = refs[0], refs[1:-2], refs[-2], refs[-1]
    a = [r[...] for r in a_refs]
    a = a[0] if len(a) == 1 else jnp.concatenate(a, axis=1)
    o_ref[...] = x_ref[...] + _dot(a, w_ref[...])


def _resid_matmul(x, acts, w, tm):
    n, d = x.shape
    return pl.pallas_call(
        _resid_matmul_kernel,
        out_shape=jax.ShapeDtypeStruct((n, d), F32),
        grid=(n // tm,),
        in_specs=[pl.BlockSpec((tm, d), lambda i: (i, 0))]
        + [pl.BlockSpec((tm, a.shape[1]), lambda i: (i, 0)) for a in acts]
        + [pl.BlockSpec(w.shape, lambda i: (0, 0))],
        out_specs=pl.BlockSpec((tm, d), lambda i: (i, 0)),
        compiler_params=_params("parallel"),
        name="resid_matmul",
    )(x, *acts, w)


def _cumsum_rows(g):
    rows = lax.broadcasted_iota(jnp.int32, g.shape, 0)
    b, sh = g, 1
    while sh < g.shape[0]:
        b = b + jnp.where(rows >= sh, pltpu.roll(b, sh, 0), 0.0)
        sh *= 2
    return b


def _hgrn_chunk(zq, zf, v, lb, st):
    c = zq.shape[0]
    f = lb + (1.0 - lb) * jax.nn.sigmoid(zf)
    g = jnp.log(jnp.maximum(f, TINY))
    kk = 1.0 - f
    q = zq * jax.nn.sigmoid(zq)
    b = _cumsum_rows(g)

    o = _dot_nt((q * jnp.exp(b)).astype(BF16), st.astype(BF16))
    b_last = b[c - 1:c]
    kdec = (kk * jnp.exp(b_last - b)).astype(BF16)
    st_new = st * jnp.exp(b_last) + _dot(v.T.astype(BF16), kdec)

    rows = lax.broadcasted_iota(jnp.int32, (HGRN_SUB, 1), 0)
    sl = TOKEN_SUBLANES
    a_rows, d_rows = [], []
    for i in range(c // HGRN_SUB):
        r0 = i * HGRN_SUB
        bi, qi, ki, vi = (t[r0:r0 + HGRN_SUB] for t in (b, q, kk, v))
        if i == 0:
            a_rows.append(jnp.zeros((HGRN_SUB, c), F32))
        else:
            beta = b[r0 - 1:r0]
            qs = (qi * jnp.exp(bi - beta)).astype(BF16)
            ks = kk[:r0] * jnp.exp(beta - b[:r0])
            ks = jnp.concatenate([ks, jnp.zeros((c - r0, ks.shape[1]), F32)], axis=0).astype(BF16)
            a_rows.append(_dot_nt(qs, ks))
        diag = [jnp.zeros((sl, v.shape[1]), F32) for _ in range(HGRN_SUB // sl)]
        for s in range(HGRN_SUB):
            lo = s // sl * sl
            w = qi[lo:] * jnp.exp(jnp.minimum(bi[lo:] - bi[s:s + 1], 0.0)) * ki[s:s + 1]
            rs = jnp.where(rows[lo:] >= s, jnp.sum(w, axis=1, keepdims=True), 0.0)
            contrib = rs * vi[s:s + 1]
            for g in range(lo // sl, HGRN_SUB // sl):
                diag[g] = diag[g] + contrib[g * sl - lo:(g + 1) * sl - lo]
        d_rows += diag
    a = jnp.concatenate(a_rows, axis=0).astype(BF16)
    o = o + _dot(a, v.astype(BF16)) + jnp.concatenate(d_rows, axis=0)
    return o, st_new


def _hgrn_kernel(zq_ref, zf_ref, zi_ref, zg_ref, lbp_ref, gn_ref, o_ref, st_ref, *, layer):
    @pl.when(pl.program_id(2) == 0)
    def _():
        st_ref[...] = jnp.zeros_like(st_ref)

    lbp = lbp_ref[...]
    e = jnp.exp(lbp - jnp.max(lbp, axis=0, keepdims=True))
    p = e / jnp.sum(e, axis=0, keepdims=True)
    lb = jnp.zeros((1, lbp.shape[1]), F32)
    for r in range(1, layer + 1):
        lb = lb + p[r:r + 1]
    gn = gn_ref[...]

    def chunk(ci, carry):
        r0 = pl.multiple_of(ci * HGRN_CHUNK, HGRN_CHUNK)
        sl = pl.ds(r0, HGRN_CHUNK)
        for hd in range(HGRN_HEADS_PER_STEP):
            cols = slice(hd * HGRN_DK, (hd + 1) * HGRN_DK)
            o, st_new = _hgrn_chunk(zq_ref[sl, cols].astype(F32), zf_ref[sl, cols].astype(F32),
                                    zi_ref[sl, cols].astype(F32), lb[:, cols], st_ref[hd])
            st_ref[hd] = st_new
            o = _rms(o, gn) * jax.nn.sigmoid(zg_ref[sl, cols].astype(F32))
            o_ref[sl, cols] = o.astype(o_ref.dtype)
        return carry

    lax.fori_loop(0, o_ref.shape[0] // HGRN_CHUNK, chunk, 0)


def _hgrn(proj, hgrn_lb, gn, layer, batch, seq):
    n = proj.shape[0]
    ns = seq // HGRN_ROWS
    hps = HGRN_HEADS_PER_STEP
    hg = HGRN_HEADS // hps
    width = hps * HGRN_DK

    def col(k):
        return pl.BlockSpec((HGRN_ROWS, width), lambda b, h, s: (b * ns + s, k * hg + h))

    return pl.pallas_call(
        functools.partial(_hgrn_kernel, layer=layer),
        out_shape=jax.ShapeDtypeStruct((n, HGRN_WIDTH), BF16),
        grid=(batch, hg, ns),
        in_specs=[col(0), col(1), col(2), col(3),
                  pl.BlockSpec((hgrn_lb.shape[0], width), lambda b, h, s: (0, h)),
                  pl.BlockSpec((1, HGRN_DK), lambda b, h, s: (0, 0))],
        out_specs=pl.BlockSpec((HGRN_ROWS, width), lambda b, h, s: (b * ns + s, h)),
        scratch_shapes=[pltpu.VMEM((hps, HGRN_DK, HGRN_DK), F32)],
        compiler_params=_params("parallel", "parallel", "arbitrary"),
        name="hgrn2",
    )(proj, proj, proj, proj, hgrn_lb, gn.reshape(1, HGRN_DK))


def _conv_kernel(cu_ref, cv_ref, dw_ref, db_ref, g_ref, b_ref, o_ref, ext_ref, rot_ref):
    rows = o_ref.shape[0]

    @pl.when(pl.program_id(1) == 0)
    def _():
        ext_ref[0:CONV_HALO, :] = jnp.zeros((CONV_HALO, ext_ref.shape[1]), F32)

    @pl.when(pl.program_id(1) > 0)
    def _():
        ext_ref[0:CONV_HALO, :] = ext_ref[rows:rows + CONV_HALO, :]

    ext_ref[CONV_HALO:, :] = cu_ref[...].astype(F32) * jax.nn.sigmoid(cv_ref[...].astype(F32))
    span = rot_ref.shape[1]
    for r in range(1, TOKEN_SUBLANES):
        rot_ref[r - 1] = ext_ref[r:r + span, :]
    acc = jnp.zeros(o_ref.shape, F32) + db_ref[...]
    first = CONV_HALO - (CONV_WIDTH - 1)
    for j in range(CONV_WIDTH):
        a, r = divmod(first + j, TOKEN_SUBLANES)
        a *= TOKEN_SUBLANES
        tap = ext_ref[a:a + rows, :] if r == 0 else rot_ref[r - 1, a:a + rows, :]
        acc = acc + dw_ref[j:j + 1, :] * tap
    mu = jnp.mean(acc, axis=-1, keepdims=True)
    cen = acc - mu
    var = jnp.mean(cen * cen, axis=-1, keepdims=True)
    y = cen * lax.rsqrt(var + EPS) * g_ref[...] + b_ref[...]
    o_ref[...] = (y * jax.nn.sigmoid(y)).astype(o_ref.dtype)


def _conv(proj, dw, db, ln_g, ln_b, batch, seq):
    n = proj.shape[0]
    ch = dw.shape[1]
    ns = seq // CONV_ROWS
    first_blk = 4 * HGRN_WIDTH // ch
    vec = pl.BlockSpec((1, ch), lambda b, s: (0, 0))
    return pl.pallas_call(
        _conv_kernel,
        out_shape=jax.ShapeDtypeStruct((n, ch), BF16),
        grid=(batch, ns),
        in_specs=[pl.BlockSpec((CONV_ROWS, ch), lambda b, s: (b * ns + s, first_blk)),
                  pl.BlockSpec((CONV_ROWS, ch), lambda b, s: (b * ns + s, first_blk + 1)),
                  pl.BlockSpec((CONV_WIDTH, ch), lambda b, s: (0, 0)), vec, vec, vec],
        out_specs=pl.BlockSpec((CONV_ROWS, ch), lambda b, s: (b * ns + s, 0)),
        scratch_shapes=[pltpu.VMEM((CONV_ROWS + CONV_HALO, ch), F32),
                        pltpu.VMEM((TOKEN_SUBLANES - 1, CONV_ROWS + CONV_HALO - TOKEN_SUBLANES, ch), F32)],
        compiler_params=_params("parallel", "arbitrary"),
        name="glu_conv_ln",
    )(proj, proj, dw, db.reshape(1, ch), ln_g.reshape(1, ch), ln_b.reshape(1, ch))


def _bias_kernel(tab_ref, o_ref):
    h = pl.program_id(0)
    t = o_ref.shape[2]
    k = lax.broadcasted_iota(jnp.int32, (t, t), 0)
    q = lax.broadcasted_iota(jnp.int32, (t, t), 1)
    max_exact = REL_BUCKETS // 2
    for d in range(2):
        rel = q - k + d * t
        nn = jnp.maximum(rel, 0)
        nf = jnp.maximum(nn, 1).astype(F32)
        large = max_exact + (jnp.log(nf / max_exact) / math.log(REL_MAX_DIST / max_exact)
                             * (REL_BUCKETS - max_exact)).astype(jnp.int32)
        large = jnp.clip(large, 0, REL_BUCKETS - 1)
        bucket = jnp.where(nn < max_exact, nn, large)
        val = jnp.zeros((t, t), F32)
        for bk in range(REL_BUCKETS):
            val = jnp.where(bucket == bk, tab_ref[bk, h], val)
        val = (val - tab_ref[REL_BUCKETS - 1, h]) * LOG2E
        if d == 0:
            val = jnp.where(rel >= 0, val, NEG_BIG)
        o_ref[0, d] = val
    o_ref[0, 2] = jnp.zeros((t, t), F32)


def _bias_tiles(rel_bias):
    assert ATT_TILE >= REL_MAX_DIST
    heads = rel_bias.shape[1]
    return pl.pallas_call(
        _bias_kernel,
        out_shape=jax.ShapeDtypeStruct((heads, 3, ATT_TILE, ATT_TILE), F32),
        grid=(heads,),
        in_specs=[pl.BlockSpec(memory_space=pltpu.SMEM)],
        out_specs=pl.BlockSpec((1, 3, ATT_TILE, ATT_TILE), lambda h: (h, 0, 0, 0)),
        compiler_params=_params("parallel"),
        name="t5_bias_tiles",
    )(rel_bias)


def _attn_kernel(q_ref, k_ref, v_ref, bias_ref, lamv_ref, g_ref, o_ref,
                 vt_ref, qs_ref, m_ref, l_ref, acc_ref, sa_ref, sb_ref, cma_ref, cmb_ref, *, lam_init):
    i = pl.program_id(2)
    t = o_ref.shape[0]

    @pl.when(i == 0)
    def _():
        for j in range(vt_ref.shape[0]):
            vt_ref[j] = v_ref[j * t:(j + 1) * t, :].astype(F32).T.astype(BF16)

    lane = lax.broadcasted_iota(jnp.int32, (1, DA_VDIM), 1)
    q = q_ref[...].astype(F32) * (DA_HEAD_DIM ** -0.5 * LOG2E)
    qs_ref[0] = jnp.where(lane < DA_HEAD_DIM, q, 0.0).astype(BF16)
    qs_ref[1] = jnp.where(lane >= DA_HEAD_DIM, q, 0.0).astype(BF16)
    m_ref[...] = jnp.full(m_ref.shape, -jnp.inf, F32)
    l_ref[...] = jnp.zeros(l_ref.shape, F32)
    acc_ref[...] = jnp.zeros(acc_ref.shape, F32)

    def produce(j, s_dst, cm_dst, far=False):
        kb = k_ref[pl.ds(pl.multiple_of(j * t, t), t), :]
        bt = None if far else bias_ref[0, jnp.minimum(i - j, 2)]
        for mp in range(2):
            s = _dot_nt(kb, qs_ref[mp])
            s = s if far else s + bt
            s_dst[mp] = s
            cm_dst[mp] = jnp.max(s, axis=0, keepdims=True)

    def consume(j, s_src, cm_src):
        vt = vt_ref[j]
        for mp in range(2):
            for st in range(t // ATT_STRIP):
                cols = slice(st * ATT_STRIP, (st + 1) * ATT_STRIP)
                m_old = m_ref[mp, :, cols]
                mn = jnp.maximum(m_old, cm_src[mp, :, cols])
                p = jnp.exp2(s_src[mp, :, cols] - mn)
                al = jnp.exp2(m_old - mn)
                m_ref[mp, :, cols] = mn
                l_ref[mp, :, cols] = l_ref[mp, :, cols] * al + jnp.sum(p, axis=0, keepdims=True)
                acc_ref[mp, :, cols] = acc_ref[mp, :, cols] * al + _dot(vt, p.astype(BF16))

    produce(0, sa_ref, cma_ref)

    def pair(pi, carry, far):
        j = 2 * pi
        produce(j + 1, sb_ref, cmb_ref, far)
        consume(j, sa_ref, cma_ref)
        produce(j + 2, sa_ref, cma_ref, far)
        consume(j + 1, sb_ref, cmb_ref)
        return carry

    n_far = jnp.maximum(i - 2, 0) >> 1
    lax.fori_loop(0, n_far, functools.partial(pair, far=True), 0)
    lax.fori_loop(n_far, i >> 1, functools.partial(pair, far=False), 0)

    @pl.when((i & 1) == 0)
    def _():
        consume(i, sa_ref, cma_ref)

    @pl.when((i & 1) == 1)
    def _():
        produce(i, sb_ref, cmb_ref)
        consume(i - 1, sa_ref, cma_ref)
        consume(i, sb_ref, cmb_ref)

    lv = lamv_ref[...]
    lam = (jnp.exp(jnp.sum(lv[0:1] * lv[1:2], axis=1, keepdims=True))
           - jnp.exp(jnp.sum(lv[2:3] * lv[3:4], axis=1, keepdims=True)) + lam_init)
    o = acc_ref[0] * (1.0 / l_ref[0]) - lam * (acc_ref[1] * (1.0 / l_ref[1]))
    y = o * lax.rsqrt(jnp.mean(o * o, axis=0, keepdims=True) + EPS) * g_ref[...] * (1.0 - lam_init)
    o_ref[...] = y.T.astype(o_ref.dtype)


def _diff_attn(qkv, bias_tiles, lamv, subln_g, lam_init, batch, seq):
    n = qkv.shape[0]
    t = ATT_TILE
    nq = seq // t
    hh = DA_HEADS
    return pl.pallas_call(
        functools.partial(_attn_kernel, lam_init=lam_init),
        out_shape=jax.ShapeDtypeStruct((n, hh * DA_VDIM), BF16),
        grid=(batch, hh, nq),
        in_specs=[pl.BlockSpec((t, DA_VDIM), lambda b, h, i: (b * nq + i, h)),
                  pl.BlockSpec((seq, DA_VDIM), lambda b, h, i: (b, hh + h)),
                  pl.BlockSpec((seq, DA_VDIM), lambda b, h, i: (b, 2 * hh + h)),
                  pl.BlockSpec((1, 3, t, t), lambda b, h, i: (h, 0, 0, 0)),
                  pl.BlockSpec(lamv.shape, lambda b, h, i: (0, 0)),
                  pl.BlockSpec((DA_VDIM, 1), lambda b, h, i: (0, 0))],
        out_specs=pl.BlockSpec((t, DA_VDIM), lambda b, h, i: (b * nq + i, h)),
        scratch_shapes=[pltpu.VMEM((nq, DA_VDIM, t), BF16), pltpu.VMEM((2, t, DA_VDIM), BF16),
                        pltpu.VMEM((2, 1, t), F32), pltpu.VMEM((2, 1, t), F32),
                        pltpu.VMEM((2, DA_VDIM, t), F32),
                        pltpu.VMEM((2, t, t), F32), pltpu.VMEM((2, t, t), F32),
                        pltpu.VMEM((2, 1, t), F32), pltpu.VMEM((2, 1, t), F32)],
        compiler_params=_params("parallel", "parallel", "arbitrary"),
        name="diff_attention",
    )(qkv, qkv, qkv, bias_tiles, lamv, subln_g.reshape(DA_VDIM, 1))


def _xattn_kernel(x_ref, g_ref, wq_ref, kv_ref, wo_ref, o_ref):
    x = x_ref[...]
    d = x.shape[1]
    dh = d // XA_HEADS
    h = _rms(x, g_ref[...]).astype(BF16)
    q = (_dot(h, wq_ref[...]) * (dh ** -0.5)).astype(BF16)
    kv = kv_ref[...]
    outs = []
    for hd in range(XA_HEADS):
        s = _dot_nt(q[:, hd * dh:(hd + 1) * dh], kv[:, hd * dh:(hd + 1) * dh])
        p = jnp.exp(s - jnp.max(s, axis=-1, keepdims=True))
        pv = _dot(p.astype(BF16), kv[:, d + hd * dh:d + (hd + 1) * dh])
        outs.append(pv * (1.0 / jnp.sum(p, axis=-1, keepdims=True)))
    o = jnp.concatenate(outs, axis=1).astype(BF16)
    o_ref[...] = x + _dot(o, wo_ref[...])


def _xattn(x, g, wq, kv, wo, seq, mem_len, tm):
    n, d = x.shape
    per_batch = seq // tm
    return pl.pallas_call(
        _xattn_kernel,
        out_shape=jax.ShapeDtypeStruct((n, d), F32),
        grid=(n // tm,),
        in_specs=[pl.BlockSpec((tm, d), lambda i: (i, 0)),
                  pl.BlockSpec((1, d), lambda i: (0, 0)),
                  pl.BlockSpec((d, d), lambda i: (0, 0)),
                  pl.BlockSpec((mem_len, 2 * d), lambda i: (i // per_batch, 0)),
                  pl.BlockSpec((d, d), lambda i: (0, 0))],
        out_specs=pl.BlockSpec((tm, d), lambda i: (i, 0)),
        compiler_params=_params("parallel"),
        name="memory_xattn",
    )(x, g.reshape(1, d), wq, kv, wo)


def _route_t(lg):
    row = lax.broadcasted_iota(jnp.int32, lg.shape, 0)
    big = lg.shape[0]
    is_grp = row < N_GROUPS
    gl = jnp.where(is_grp, lg, -jnp.inf)
    gmax = jnp.max(gl, axis=0, keepdims=True)
    gsel = jnp.min(jnp.where(gl == gmax, row, big), axis=0, keepdims=True)
    g_w = 1.0 / jnp.sum(jnp.where(is_grp, jnp.exp(lg - gmax), 0.0), axis=0, keepdims=True)
    e_idx = row - N_GROUPS
    in_grp = (e_idx >= 0) & (e_idx < N_EXPERTS) & (e_idx // EXPERTS_PER_GROUP == gsel)
    ml = jnp.where(in_grp, lg, -jnp.inf)
    v1 = jnp.max(ml, axis=0, keepdims=True)
    i1 = jnp.min(jnp.where(ml == v1, row, big), axis=0, keepdims=True)
    ml2 = jnp.where(row == i1, -jnp.inf, ml)
    v2 = jnp.max(ml2, axis=0, keepdims=True)
    i2 = jnp.min(jnp.where(ml2 == v2, row, big), axis=0, keepdims=True)
    t = jnp.exp(v2 - v1)
    w1 = g_w / (1.0 + t)
    w2 = g_w * t / (1.0 + t)
    return gsel, jnp.where(row == i1, w1, 0.0) + jnp.where(row == i2, w2, 0.0)


def _router_kernel(x_ref, g_ref, wrt_ref, brt_ref, tok_ref, idx_ref, cnt_ref, tri_ref, run_ref):
    t, d = x_ref.shape

    @pl.when(pl.program_id(0) == 0)
    def _():
        r = lax.broadcasted_iota(jnp.int32, (t, t), 0)
        c = lax.broadcasted_iota(jnp.int32, (t, t), 1)
        tri_ref[...] = (r < c).astype(BF16)
        run_ref[...] = jnp.zeros_like(run_ref)

    h = _rms(x_ref[...], g_ref[...])
    logits = lax.dot_general(wrt_ref[...], h, (((1,), (1,)), ((), ())), preferred_element_type=F32,
                             precision=lax.Precision.HIGHEST) + brt_ref[...]
    gsel, gate = _route_t(logits[0:ROUTE_ROWS])

    row8 = lax.broadcasted_iota(jnp.int32, (8, t), 0)
    memb = (row8 == gsel).astype(F32)
    before = _dot(memb.astype(BF16), tri_ref[...]) + run_ref[...]
    pos = jnp.sum(memb * before, axis=0, keepdims=True)
    run_ref[...] = run_ref[...] + jnp.sum(memb, axis=1, keepdims=True)
    cnt_ref[...] = run_ref[:, 0:LANES]
    idx_ref[0] = jnp.concatenate([gsel.astype(F32), pos, jnp.zeros((6, t), F32)], axis=0)

    bits = lax.bitcast_convert_type(h.astype(BF16).astype(F32), jnp.uint32)
    words = (bits[:, 0:d // 2] >> 16) | bits[:, d // 2:]
    h_sublanes = d // 2 // LANES
    for s in range(h_sublanes):
        tok_ref[_token_rows(s, t)] = words[:, s * LANES:(s + 1) * LANES]
    g16 = gate[N_GROUPS:N_GROUPS + N_EXPERTS]
    gate_rows = jnp.concatenate([g16, jnp.zeros((LANES - N_EXPERTS, t), F32)], axis=0).T
    tok_ref[_token_rows(h_sublanes, t)] = lax.bitcast_convert_type(gate_rows, jnp.uint32)
    for s in range(h_sublanes + 1, TOKEN_SUBLANES):
        tok_ref[_token_rows(s, t)] = jnp.zeros((t, LANES), jnp.uint32)


def _router(x, g, wrt, brt, tm):
    n, d = x.shape
    nt = n // tm
    assert d // 2 // LANES + 1 <= TOKEN_SUBLANES
    return pl.pallas_call(
        _router_kernel,
        out_shape=(jax.ShapeDtypeStruct((n * TOKEN_SUBLANES, LANES), jnp.uint32),
                   jax.ShapeDtypeStruct((nt, 8, tm), F32),
                   jax.ShapeDtypeStruct((8, LANES), F32)),
        grid=(nt,),
        in_specs=[pl.BlockSpec((tm, d), lambda i: (i, 0)),
                  pl.BlockSpec((1, d), lambda i: (0, 0)),
                  pl.BlockSpec((LANES, d), lambda i: (0, 0)),
                  pl.BlockSpec((LANES, 1), lambda i: (0, 0))],
        out_specs=(pl.BlockSpec((tm * TOKEN_SUBLANES, LANES), lambda i: (i, 0)),
                   pl.BlockSpec((1, 8, tm), lambda i: (i, 0, 0)),
                   pl.BlockSpec((8, LANES), lambda i: (0, 0))),
        scratch_shapes=[pltpu.VMEM((tm, tm), BF16), pltpu.VMEM((8, tm), F32)],
        compiler_params=_params("arbitrary"),
        name="moe_router",
    )(x, g.reshape(1, d), wrt, brt)


def _permute_kernel(idx_ref, src_ref, *rest, scatter):
    dst_ref, sem = rest[-2], rest[-1]
    rows = idx_ref.shape[2]

    def copy(r):
        j = idx_ref[0, 0, r]
        src_row, dst_row = (r, j) if scatter else (j, r)
        return pltpu.make_async_copy(src_ref.at[pl.ds(src_row, 1)], dst_ref.at[pl.ds(dst_row, 1)], sem)

    def start(k, carry):
        for u in range(PERMUTE_UNROLL):
            copy(k * PERMUTE_UNROLL + u).start(priority=u % 2)
        return carry

    def wait(r, carry):
        copy(r).wait()
        return carry

    lax.fori_loop(0, rows // PERMUTE_UNROLL, start, 0)
    lax.fori_loop(0, rows, wait, 0, unroll=PERMUTE_UNROLL)


def _permute_tokens(idx, src, dst_tokens, scatter, init=None):
    n = idx.shape[0]
    rows = PERMUTE_ROWS
    tile = (TOKEN_SUBLANES, LANES)
    any_spec = pl.BlockSpec(memory_space=pl.ANY)
    block = pl.BlockSpec((rows, *tile), lambda i: (i, 0, 0))
    ins = [idx.reshape(n // rows, 1, rows), src.reshape(-1, *tile)]
    if init is not None:
        ins.append(init.reshape(-1, *tile))
    out = pl.pallas_call(
        functools.partial(_permute_kernel, scatter=scatter),
        out_shape=jax.ShapeDtypeStruct((dst_tokens, *tile), src.dtype),
        grid=(n // rows,),
        in_specs=[pl.BlockSpec((1, 1, rows), lambda i: (i, 0, 0), memory_space=pltpu.SMEM),
                  block if scatter else any_spec] + ([any_spec] if init is not None else []),
        out_specs=any_spec if scatter else block,
        scratch_shapes=[pltpu.SemaphoreType.DMA],
        input_output_aliases={2: 0} if init is not None else {},
        compiler_params=_params("arbitrary"),
        name="moe_scatter_tokens" if scatter else "moe_gather_tokens",
    )(*ins)
    return out.reshape(dst_tokens * TOKEN_SUBLANES, LANES)


def _experts_kernel(grp_ref, nblk_ref, tok_ref, w1_ref, w3_ref, w2_ref, o_ref):
    i = pl.program_id(0)
    tb = o_ref.shape[0] // TOKEN_SUBLANES
    d = w1_ref.shape[2]
    h_sublanes = d // 2 // LANES

    @pl.when(i < nblk_ref[0])
    def _():
        words = [tok_ref[_token_rows(s, tb)] for s in range(h_sublanes)]
        low = [lax.bitcast_convert_type(w << 16, F32).astype(BF16) for w in words]
        high = [lax.bitcast_convert_type(w & jnp.uint32(0xFFFF0000), F32).astype(BF16) for w in words]
        h = jnp.concatenate(low + high, axis=1)
        gates = lax.bitcast_convert_type(tok_ref[_token_rows(h_sublanes, tb)], F32)
        lane = lax.broadcasted_iota(jnp.int32, (1, LANES), 1)
        first = grp_ref[i] * EXPERTS_PER_GROUP
        acc = jnp.zeros((tb, d), F32)
        for e in range(EXPERTS_PER_GROUP):
            gate = jnp.sum(jnp.where(lane == first + e, gates, 0.0), axis=1, keepdims=True)
            h1 = _dot(h, w1_ref[0, e])
            hid = h1 * jax.nn.sigmoid(h1) * _dot(h, w3_ref[0, e]) * gate
            acc = acc + _dot(hid.astype(BF16), w2_ref[0, e])
        for s in range(TOKEN_SUBLANES):
            o_ref[_token_rows(s, tb)] = acc[:, s * LANES:(s + 1) * LANES]

    @pl.when(i >= nblk_ref[0])
    def _():
        o_ref[...] = jnp.zeros_like(o_ref)


def _experts(toks, blk_grp, nblk, w1, w3, w2):
    d, ff = w1.shape[-2:]
    assert d == TOKEN_SUBLANES * LANES
    tb = MOE_BLOCK
    epg = EXPERTS_PER_GROUP
    tile_rows = pl.BlockSpec((tb * TOKEN_SUBLANES, LANES), lambda i, grp, nb: (i, 0))
    grid_spec = pltpu.PrefetchScalarGridSpec(
        num_scalar_prefetch=2,
        grid=(toks.shape[0] // (tb * TOKEN_SUBLANES),),
        in_specs=[tile_rows,
                  pl.BlockSpec((1, epg, d, ff), lambda i, grp, nb: (grp[i], 0, 0, 0)),
                  pl.BlockSpec((1, epg, d, ff), lambda i, grp, nb: (grp[i], 0, 0, 0)),
                  pl.BlockSpec((1, epg, ff, d), lambda i, grp, nb: (grp[i], 0, 0, 0))],
        out_specs=tile_rows,
    )
    return pl.pallas_call(
        _experts_kernel,
        out_shape=jax.ShapeDtypeStruct(toks.shape, F32),
        grid_spec=grid_spec,
        compiler_params=_params("arbitrary"),
        name="moe_experts",
    )(blk_grp, nblk, toks, w1.reshape(N_GROUPS, epg, d, ff), w3.reshape(N_GROUPS, epg, d, ff),
      w2.reshape(N_GROUPS, epg, ff, d))


def _moe(x, g, w_grp, b_grp, w_exp, b_exp, w1, w3, w2):
    n, d = x.shape
    pad = LANES - N_GROUPS - N_EXPERTS
    wrt = jnp.concatenate([w_grp, w_exp, jnp.zeros((d, pad), F32)], axis=1).T
    brt = jnp.concatenate([b_grp, b_exp, jnp.zeros((pad,), F32)]).reshape(LANES, 1)
    toks, idx, cnt = _router(x, g, wrt, brt, MOE_ROWS)

    tb = MOE_BLOCK
    counts = cnt[0:N_GROUPS, 0].astype(jnp.int32)
    padded = (counts + tb - 1) // tb * tb
    ends = jnp.cumsum(padded)
    offs = ends - padded
    gsel = idx[:, 0, :].reshape(n).astype(jnp.int32)
    group_ids = jnp.arange(N_GROUPS, dtype=jnp.int32)
    dest = (jnp.sum(jnp.where(gsel[:, None] == group_ids[None, :], offs[None, :], 0), axis=1)
            + idx[:, 1, :].reshape(n).astype(jnp.int32))
    nblk_max = n // tb + N_GROUPS
    blk_start = jnp.arange(nblk_max, dtype=jnp.int32) * tb
    blk_grp = jnp.minimum(jnp.sum(blk_start[:, None] >= ends[None, :], axis=1), N_GROUPS - 1).astype(jnp.int32)
    nblk = (ends[N_GROUPS - 1:] // tb).astype(jnp.int32)

    sorted_toks = _permute_tokens(dest, toks, nblk_max * tb, True,
                                  init=jnp.zeros((nblk_max * tb * TOKEN_SUBLANES, LANES), jnp.uint32))
    ys = _experts(sorted_toks, blk_grp, nblk, w1, w3, w2)
    return _permute_tokens(dest, ys, n, False)


def _final_norm_kernel(x_ref, y_ref, g_ref, o_ref):
    o_ref[...] = _rms(x_ref[...] + _from_token_tiles(y_ref), g_ref[...])


def _final_norm(x, y, g, tm):
    n, d = x.shape
    row = pl.BlockSpec((tm, d), lambda i: (i, 0))
    return pl.pallas_call(
        _final_norm_kernel,
        out_shape=jax.ShapeDtypeStruct((n, d), F32),
        grid=(n // tm,),
        in_specs=[row, pl.BlockSpec((tm * TOKEN_SUBLANES, LANES), lambda i: (i, 0)),
                  pl.BlockSpec((1, d), lambda i: (0, 0))],
        out_specs=row,
        compiler_params=_params("parallel"),
        name="final_norm",
    )(x, y, g.reshape(1, d))


def kernel(x, mem, rel_bias, mem_norm_g, final_norm_g, hgrn_lb, norm_mix_g, norm_xattn_g, norm_ffn_g, even_w_in, even_w_out, hgrn_norm_g, conv_dw, conv_db, conv_ln_g, conv_ln_b, odd_w_qkv, odd_w_out, lam_q1, lam_k1, lam_q2, lam_k2, subln_g, xa_wq, xa_wkv, xa_wo, moe_w_grp, moe_b_grp, moe_w_exp, moe_b_exp, moe_w1, moe_w3, moe_w2):
    batch, seq, d = x.shape
    mem_len = mem.shape[1]
    depth = norm_mix_g.shape[0]
    n = batch * seq
    assert seq % ATT_TILE == 0 and seq % HGRN_ROWS == 0 and seq % CONV_ROWS == 0 and seq % ROW_TILE == 0
    assert n % MOE_ROWS == 0 and n % MOE_BLOCK == 0 and n % PERMUTE_ROWS == 0
    assert (batch * mem_len) % ROW_TILE == 0

    xf = x.reshape(n, d)
    memf = mem.reshape(batch * mem_len, d)
    bias_tiles = _bias_tiles(rel_bias)
    y_moe = None

    for i in range(depth):
        j = i // 2
        w_in = (even_w_in if i % 2 == 0 else odd_w_qkv)[j].astype(BF16)
        if y_moe is None:
            proj = _norm_matmul(xf, norm_mix_g[i], w_in, ROW_TILE)
        else:
            xf, proj = _add_norm_matmul(xf, y_moe, norm_mix_g[i], w_in, ROW_TILE)
        if i % 2 == 0:
            o_hgrn = _hgrn(proj, hgrn_lb, hgrn_norm_g[j], j, batch, seq)
            o_conv = _conv(proj, conv_dw[j], conv_db[j], conv_ln_g[j], conv_ln_b[j], batch, seq)
            xf = _resid_matmul(xf, [o_hgrn, o_conv], even_w_out[j].astype(BF16), ROW_TILE)
        else:
            lam_init = 0.8 - 0.6 * math.exp(-0.3 * i)
            lamv = jnp.stack([lam_q1[j], lam_k1[j], lam_q2[j], lam_k2[j]]).astype(F32)
            o_att = _diff_attn(proj, bias_tiles, lamv, subln_g[j], lam_init, batch, seq)
            xf = _resid_matmul(xf, [o_att], odd_w_out[j].astype(BF16), ROW_TILE)

        kv = _norm_matmul(memf, mem_norm_g, xa_wkv[i].astype(BF16), ROW_TILE)
        xf = _xattn(xf, norm_xattn_g[i], xa_wq[i].astype(BF16), kv, xa_wo[i].astype(BF16), seq, mem_len, ROW_TILE)

        y_moe = _moe(xf, norm_ffn_g[i], moe_w_grp[i], moe_b_grp[i], moe_w_exp[i], moe_b_exp[i],
                     moe_w1[i].astype(BF16), moe_w3[i].astype(BF16), moe_w2[i].astype(BF16))

    return _final_norm(xf, y_moe, final_norm_g, ROW_TILE).reshape(batch, seq, d)
```
